```python
import jax, jax.numpy as jnp
from jax import lax
import numpy as np

D_MODEL = 2048
BATCH = 8
SEQ = 2048
DEPTH = 1

EPS = 1e-6
DA_HEADS = 8
DA_QK_DIM = 64
DA_V_DIM = 2 * DA_QK_DIM
ROPE_DIM = DA_QK_DIM // 4
ROPE_THETA = 500000.0
Q_BLOCK = 128
HG_HEADS = 8
HG_K_DIM = 128
HG_V_DIM = 128
HG_CHUNK = 64
DA_QK_W = DA_HEADS * 2 * DA_QK_DIM
DA_V_W = DA_HEADS * DA_V_DIM
HG_K_W = HG_HEADS * HG_K_DIM
HG_V_W = HG_HEADS * HG_V_DIM
SPLIT_SIZES = (DA_QK_W, DA_QK_W, DA_V_W, HG_K_W, HG_K_W, HG_K_W, HG_V_W, HG_V_W, D_MODEL, D_MODEL)
IN_W = sum(SPLIT_SIZES)
SPLIT_IDX = tuple(int(i) for i in np.cumsum(SPLIT_SIZES)[:-1])
PEER_HEADS = 8
PEER_N_KEYS = 128
PEER_N_EXPERTS = PEER_N_KEYS * PEER_N_KEYS
PEER_KEY_DIM = 256
PEER_HALF = PEER_KEY_DIM // 2
PEER_TOPK = 16
PEER_TOKEN_BLOCK = 128

kernel_name = "cond_hybrid_diffattn_hgrn2_peer_encoder"


def _rmsnorm(x, g):
    xf = x.astype(jnp.float32)
    y = xf * lax.rsqrt(jnp.mean(xf * xf, axis=-1, keepdims=True) + EPS)
    return (y * g.astype(jnp.float32)).astype(x.dtype)


def _modulate(x, g, shift, scale):
    return _rmsnorm(x, g) * (1.0 + scale[:, None, :]) + shift[:, None, :]


def _rope_tables(positions):
    inv = ROPE_THETA ** (-jnp.arange(0, ROPE_DIM, 2, dtype=jnp.float32) / ROPE_DIM)
    ang = positions.astype(jnp.float32)[..., None] * inv
    return jnp.cos(ang), jnp.sin(ang)


def _partial_rope(t, cos, sin):
    half = ROPE_DIM // 2
    c = cos[:, :, None, None, :].astype(t.dtype)
    s = sin[:, :, None, None, :].astype(t.dtype)
    x1, x2, rest = t[..., :half], t[..., half:ROPE_DIM], t[..., ROPE_DIM:]
    return jnp.concatenate([x1 * c - x2 * s, x2 * c + x1 * s, rest], axis=-1)


def _diff_attention(q, k, v, lam, lambda_init, subln_g):
    B, S, H, _, dk = q.shape
    nblk = S // Q_BLOCK
    q = q * (dk ** -0.5)
    qb = q.reshape(B, nblk, Q_BLOCK, H, 2, dk).transpose(1, 0, 2, 3, 4, 5)

    def block(qblk):
        s = jnp.einsum('bqhmd,bkhmd->bhmqk', qblk, k).astype(jnp.float32)
        p = jax.nn.softmax(s, axis=-1)
        p = p[:, :, 0] - lam * p[:, :, 1]
        return jnp.einsum('bhqk,bkhd->bqhd', p.astype(v.dtype), v)

    o = lax.map(block, qb)
    o = o.transpose(1, 0, 2, 3, 4).reshape(B, S, H, v.shape[-1])
    o = _rmsnorm(o, subln_g) * (1.0 - lambda_init)
    return o.reshape(B, S, H * v.shape[-1])


def _gla_chunkwise(q, k, v, log_f):
    B, S, H, dk = q.shape
    dv = v.shape[-1]
    L = HG_CHUNK
    C = S // L
    to_c = lambda t: t.reshape(B, C, L, H, t.shape[-1]).transpose(0, 3, 1, 2, 4)
    q, k, v, log_f = to_c(q), to_c(k), to_c(v), to_c(log_f)
    b = jnp.cumsum(log_f, axis=3)
    b_last = b[:, :, :, -1:, :]
    r = b[:, :, :, L // 2 - 1:L // 2, :]
    qi = q * jnp.exp(b - r)
    ki = k * jnp.exp(r - b)
    A = jnp.einsum('bhctd,bhcsd->bhcts', qi, ki)
    mask = jnp.tril(jnp.ones((L, L), dtype=bool))
    A = jnp.where(mask, A, 0.0)
    o_intra = jnp.einsum('bhcts,bhcsv->bhctv', A, v)
    U = jnp.einsum('bhcsd,bhcsv->bhcdv', k * jnp.exp(b_last - b), v)
    a = jnp.exp(b_last[:, :, :, 0, :])

    def step(state, inp):
        a_c, U_c = inp
        return a_c[..., None] * state + U_c, state

    s0 = jnp.zeros((B, H, dk, dv), jnp.float32)
    _, s_prev = lax.scan(step, s0, (jnp.moveaxis(a, 2, 0), jnp.moveaxis(U, 2, 0)))
    s_prev = jnp.moveaxis(s_prev, 0, 2)
    o_inter = jnp.einsum('bhctd,bhcdv->bhctv', q * jnp.exp(b), s_prev)
    o = o_intra + o_inter
    return o.transpose(0, 2, 3, 1, 4).reshape(B, S, H, dv)


def _layer_lower_bound(tbl, l):
    sm = jax.nn.softmax(tbl.astype(jnp.float32), axis=0)
    cum = jnp.cumsum(sm, axis=0)
    return (cum[l + 1] - cum[0]).reshape(HG_HEADS, HG_K_DIM)


def _hgrn2_gates(f_raw, lb):
    f = lb + (1.0 - lb) * jax.nn.sigmoid(f_raw)
    return 1.0 - f, jnp.log(f)


def _hgrn2(q_raw, ff_raw, fb_raw, i_raw, g_raw, lb_f, lb_b, norm_g):
    B, S, _ = q_raw.shape
    hk = lambda t: t.astype(jnp.float32).reshape(B, S, HG_HEADS, -1)
    q = jax.nn.silu(hk(q_raw))
    v = hk(i_raw)
    k_f, logf_f = _hgrn2_gates(hk(ff_raw), lb_f)
    k_b, logf_b = _hgrn2_gates(hk(fb_raw), lb_b)
    o_f = _gla_chunkwise(q, k_f, v, logf_f)
    flip = lambda t: jnp.flip(t, axis=1)
    o_b = flip(_gla_chunkwise(flip(q), flip(k_b), flip(v), flip(logf_b)))
    o = _rmsnorm(o_f + o_b, norm_g) * jax.nn.silu(hk(g_raw))
    return o.reshape(B, S, HG_V_W).astype(q_raw.dtype)


def _peer(h, wq, keys1, keys2, u, v):
    B, S, D = h.shape
    T = B * S
    t = h.reshape(T, D)
    qp = (t @ wq).reshape(T, PEER_HEADS, 2, PEER_HALF)
    s1 = jnp.einsum('thk,hnk->thn', qp[:, :, 0], keys1).astype(jnp.float32)
    s2 = jnp.einsum('thk,hnk->thn', qp[:, :, 1], keys2).astype(jnp.float32)
    v1, i1 = lax.top_k(s1, PEER_TOPK)
    v2, i2 = lax.top_k(s2, PEER_TOPK)
    cand = (v1[..., :, None] + v2[..., None, :]).reshape(T, PEER_HEADS, PEER_TOPK * PEER_TOPK)
    sc, ci = lax.top_k(cand, PEER_TOPK)
    e = (jnp.take_along_axis(i1, ci // PEER_TOPK, axis=-1) * PEER_N_KEYS
         + jnp.take_along_axis(i2, ci % PEER_TOPK, axis=-1))
    g = jax.nn.softmax(sc, axis=-1)
    nb = T // PEER_TOKEN_BLOCK
    kk = PEER_HEADS * PEER_TOPK
    xb = t.reshape(nb, PEER_TOKEN_BLOCK, D)
    eb = e.reshape(nb, PEER_TOKEN_BLOCK, kk)
    gb = g.reshape(nb, PEER_TOKEN_BLOCK, kk).astype(h.dtype)

    def blk(args):
        xt, et, gt = args
        us = jnp.take(u, et, axis=0)
        a = jax.nn.gelu(jnp.einsum('td,tkd->tk', xt, us), approximate=False) * gt
        vs = jnp.take(v, et, axis=0)
        return jnp.einsum('tk,tkd->td', a, vs)

    return lax.map(blk, (xb, eb, gb)).reshape(B, S, D)


def setup_inputs(seed: int = 0) -> dict:
    key = jax.random.key(seed)
    ks = jax.random.split(key, 26)
    f32 = jnp.float32
    nrm = lambda k, shape, s: jax.random.normal(k, shape, f32) * s
    gain = lambda k, shape: 1.0 + 0.02 * jax.random.normal(k, shape, f32)
    L, D = DEPTH, D_MODEL
    return {
        "x": nrm(ks[0], (BATCH, SEQ, D), 1.0),
        "c": nrm(ks[1], (BATCH, D), 1.0),
        "positions": jnp.broadcast_to(jnp.arange(SEQ, dtype=jnp.int32), (BATCH, SEQ)),
        "w_ada": nrm(ks[2], (L, D, 6 * D), 0.5 * D ** -0.5),
        "b_ada": nrm(ks[3], (L, 6 * D), 0.01),
        "norm1_g": gain(ks[4], (L, D)),
        "w_in": nrm(ks[5], (L, D, IN_W), D ** -0.5),
        "diff_lq1": nrm(ks[6], (L, DA_QK_DIM), 0.1),
        "diff_lk1": nrm(ks[7], (L, DA_QK_DIM), 0.1),
        "diff_lq2": nrm(ks[8], (L, DA_QK_DIM), 0.1),
        "diff_lk2": nrm(ks[9], (L, DA_QK_DIM), 0.1),
        "diff_subln_g": gain(ks[10], (L, DA_V_DIM)),
        "hgrn_lb_fwd": nrm(ks[11], (L + 1, HG_K_W), 0.1),
        "hgrn_lb_bwd": nrm(ks[12], (L + 1, HG_K_W), 0.1),
        "hgrn_norm_g": gain(ks[13], (L, HG_V_DIM)),
        "w_branch_attn": nrm(ks[14], (L, DA_V_W, D), DA_V_W ** -0.5),
        "w_branch_hgrn": nrm(ks[15], (L, HG_V_W, D), HG_V_W ** -0.5),
        "w_out": nrm(ks[16], (L, D, D), D ** -0.5),
        "norm2_g": gain(ks[17], (L, D)),
        "peer_wq": nrm(ks[18], (L, D, PEER_HEADS * PEER_KEY_DIM), D ** -0.5),
        "peer_keys1": nrm(ks[19], (L, PEER_HEADS, PEER_N_KEYS, PEER_HALF), PEER_HALF ** -0.5),
        "peer_keys2": nrm(ks[20], (L, PEER_HEADS, PEER_N_KEYS, PEER_HALF), PEER_HALF ** -0.5),
        "peer_u": nrm(ks[21], (L, PEER_N_EXPERTS, D), D ** -0.5),
        "peer_v": nrm(ks[22], (L, PEER_N_EXPERTS, D), PEER_HEADS ** -0.5),
        "final_norm_g": gain(ks[23], (D,)),
    }


def reference(x, c, positions, w_ada, b_ada, norm1_g, w_in, diff_lq1, diff_lk1, diff_lq2, diff_lk2,
              diff_subln_g, hgrn_lb_fwd, hgrn_lb_bwd, hgrn_norm_g, w_branch_attn, w_branch_hgrn,
              w_out, norm2_g, peer_wq, peer_keys1, peer_keys2, peer_u, peer_v, final_norm_g):
    B, S, D = x.shape
    cos, sin = _rope_tables(positions)
    for l in range(DEPTH):
        ada = jax.nn.silu(c) @ w_ada[l] + b_ada[l]
        shift1, scale1, gate1, shift2, scale2, gate2 = jnp.split(ada, 6, axis=-1)
        h = _modulate(x, norm1_g[l], shift1, scale1)
        parts = jnp.split(h @ w_in[l], SPLIT_IDX, axis=-1)
        da_q, da_k, da_v, hg_q, hg_ff, hg_fb, hg_i, hg_g, g_a, g_b = parts
        qa = _partial_rope(da_q.reshape(B, S, DA_HEADS, 2, DA_QK_DIM), cos, sin)
        ka = _partial_rope(da_k.reshape(B, S, DA_HEADS, 2, DA_QK_DIM), cos, sin)
        va = da_v.reshape(B, S, DA_HEADS, DA_V_DIM)
        lambda_init = 0.8 - 0.6 * float(np.exp(-0.3 * l))
        lam = (jnp.exp(jnp.sum(diff_lq1[l].astype(jnp.float32) * diff_lk1[l].astype(jnp.float32)))
               - jnp.exp(jnp.sum(diff_lq2[l].astype(jnp.float32) * diff_lk2[l].astype(jnp.float32)))
               + lambda_init)
        o_a = _diff_attention(qa, ka, va, lam, lambda_init, diff_subln_g[l])
        o_b = _hgrn2(hg_q, hg_ff, hg_fb, hg_i, hg_g,
                     _layer_lower_bound(hgrn_lb_fwd, l), _layer_lower_bound(hgrn_lb_bwd, l),
                     hgrn_norm_g[l])
        y = (jax.nn.sigmoid(g_a) * (o_a @ w_branch_attn[l])
             + jax.nn.sigmoid(g_b) * (o_b @ w_branch_hgrn[l]))
        x = x + gate1[:, None, :] * (y @ w_out[l])
        h2 = _modulate(x, norm2_g[l], shift2, scale2)
        x = x + gate2[:, None, :] * _peer(h2, peer_wq[l], peer_keys1[l], peer_keys2[l], peer_u[l], peer_v[l])
    return _rmsnorm(x, final_norm_g)
```

```python
import functools

import numpy as np
import jax
import jax.numpy as jnp
from jax import lax
from jax.experimental import pallas as pl
from jax.experimental.pallas import tpu as pltpu

F32 = jnp.float32
BF16 = jnp.bfloat16

EPS = 1e-6
DA_HEADS = 8
DA_QK_DIM = 64
DA_V_DIM = 128
ROPE_DIM = DA_QK_DIM // 4
ROPE_HALF = ROPE_DIM // 2
ROPE_THETA = 500000.0
HG_HEADS = 8
HG_DIM = 128
HG_CHUNK = 64
PEER_HEADS = 8
PEER_N_KEYS = 128
PEER_HALF = 128
PEER_TOPK = 16
PEER_KK = PEER_HEADS * PEER_TOPK

LANES = 128
VMEM_LIMIT = 56 * 1024 * 1024

COL_DA_Q, COL_DA_K, COL_DA_V = 0, 8, 16
COL_HG_Q, COL_HG_FF, COL_HG_FB, COL_HG_I, COL_HG_G = 24, 32, 40, 48, 56

_NT = (((1,), (1,)), ((), ()))
_TN = (((0,), (0,)), ((), ()))


def _params(*sem):
    return pltpu.CompilerParams(dimension_semantics=sem, vmem_limit_bytes=VMEM_LIMIT)


def _silu(x):
    return x * jax.nn.sigmoid(x)


def _ada_kernel(c_ref, w_ref, b_ref, o_ref):
    s = _silu(c_ref[...])
    s_hi = s.astype(BF16)
    s_lo = (s - s_hi.astype(F32)).astype(BF16)
    w = w_ref[...]
    w_hi = w.astype(BF16)
    w_lo = (w - w_hi.astype(F32)).astype(BF16)
    acc = jnp.dot(s_hi, w_hi, preferred_element_type=F32)
    acc += jnp.dot(s_lo, w_hi, preferred_element_type=F32)
    acc += jnp.dot(s_hi, w_lo, preferred_element_type=F32)
    o_ref[...] = acc + b_ref[...]


def _ada(c, w, b):
    B, D = c.shape
    N = w.shape[1]
    tn = 1024
    return pl.pallas_call(
        _ada_kernel,
        out_shape=jax.ShapeDtypeStruct((B, N), F32),
        grid=(N // tn,),
        in_specs=[pl.BlockSpec((B, D), lambda j: (0, 0)),
                  pl.BlockSpec((D, tn), lambda j: (0, j)),
                  pl.BlockSpec((1, tn), lambda j: (0, j))],
        out_specs=pl.BlockSpec((B, tn), lambda j: (0, j)),
        compiler_params=_params("arbitrary"),
        name="ada",
    )(c, w, b.reshape(1, N))


def _modulated_norm(xf, g, shift, scale):
    y = xf * lax.rsqrt(jnp.mean(xf * xf, axis=-1, keepdims=True) + EPS) * g
    return y * (1.0 + scale) + shift


def _proj_kernel(x_ref, shift_ref, scale_ref, g_ref, w_ref, rc_ref, rs1_ref, rs2_ref, o_ref, h_ref):
    j = pl.program_id(1)

    @pl.when(j == 0)
    def _():
        h_ref[...] = _modulated_norm(x_ref[...], g_ref[...], shift_ref[0], scale_ref[0]).astype(BF16)

    acc = jnp.dot(h_ref[...], w_ref[...], preferred_element_type=F32)
    tn = acc.shape[1]

    def store_rotated(mult):
        rc, rs1, rs2 = rc_ref[...], rs1_ref[...], rs2_ref[...]
        for cb in range(tn // LANES):
            t = acc[:, cb * LANES:(cb + 1) * LANES]
            r = (t * rc + pltpu.roll(t, ROPE_HALF, axis=1) * rs1
                 + pltpu.roll(t, LANES - ROPE_HALF, axis=1) * rs2)
            o_ref[:, cb * LANES:(cb + 1) * LANES] = (r * mult).astype(o_ref.dtype)

    @pl.when(j == 0)
    def _():
        store_rotated(DA_QK_DIM ** -0.5)

    @pl.when(j == 1)
    def _():
        store_rotated(1.0)

    @pl.when(j >= 2)
    def _():
        o_ref[...] = acc.astype(o_ref.dtype)


def _proj(x2d, shift, scale, g, w_bf16, rc, rs1, rs2, seq):
    T, D = x2d.shape
    N = w_bf16.shape[1]
    tm, tn = min(1024, seq), 1024
    assert seq % tm == 0 and N % tn == 0 and tn == DA_HEADS * 2 * DA_QK_DIM
    bidx = lambda i, j: ((i * tm) // seq, 0, 0)
    return pl.pallas_call(
        _proj_kernel,
        out_shape=jax.ShapeDtypeStruct((T, N), BF16),
        grid=(T // tm, N // tn),
        in_specs=[pl.BlockSpec((tm, D), lambda i, j: (i, 0)),
                  pl.BlockSpec((1, 1, D), bidx),
                  pl.BlockSpec((1, 1, D), bidx),
                  pl.BlockSpec((1, D), lambda i, j: (0, 0)),
                  pl.BlockSpec((D, tn), lambda i, j: (0, j)),
                  pl.BlockSpec((tm, LANES), lambda i, j: (i, 0)),
                  pl.BlockSpec((tm, LANES), lambda i, j: (i, 0)),
                  pl.BlockSpec((tm, LANES), lambda i, j: (i, 0))],
        out_specs=pl.BlockSpec((tm, tn), lambda i, j: (i, j)),
        scratch_shapes=[pltpu.VMEM((tm, D), BF16)],
        compiler_params=_params("arbitrary", "arbitrary"),
        name="proj",
    )(x2d, shift, scale, g, w_bf16, rc, rs1, rs2)


def _rope_lane_tables(positions):
    B, S = positions.shape
    T = B * S
    inv = ROPE_THETA ** (-jnp.arange(0, ROPE_DIM, 2, dtype=F32) / ROPE_DIM)
    ang = positions.astype(F32).reshape(T, 1) * inv
    cos, sin = jnp.cos(ang), jnp.sin(ang)
    one = jnp.ones((T, DA_QK_DIM - ROPE_DIM), F32)
    zero8 = jnp.zeros((T, ROPE_HALF), F32)
    zero48 = jnp.zeros((T, DA_QK_DIM - ROPE_DIM), F32)
    rc = jnp.concatenate([cos, cos, one], axis=1)
    rs1 = jnp.concatenate([zero8, sin, zero48], axis=1)
    rs2 = jnp.concatenate([-sin, zero8, zero48], axis=1)
    rep = LANES // DA_QK_DIM
    return jnp.tile(rc, (1, rep)), jnp.tile(rs1, (1, rep)), jnp.tile(rs2, (1, rep))


def _attn_kernel(lam_ref, q_ref, k_ref, v_ref, g_ref, o_ref, *, out_scale):
    q = q_ref[...]
    k = k_ref[...]
    v = v_ref[...]
    lane = lax.broadcasted_iota(jnp.int32, q.shape, 1)
    zero = jnp.zeros_like(q)

    def one_map(qm):
        s = lax.dot_general(qm, k, _NT, preferred_element_type=F32)
        e = jnp.exp(s - jnp.max(s, axis=-1, keepdims=True))
        l = jnp.sum(e, axis=-1, keepdims=True)
        return jnp.dot(e.astype(v.dtype), v, preferred_element_type=F32) / l

    o1 = one_map(jnp.where(lane < DA_QK_DIM, q, zero))
    o2 = one_map(jnp.where(lane >= DA_QK_DIM, q, zero))
    o = o1 - lam_ref[0] * o2
    o = o * lax.rsqrt(jnp.mean(o * o, axis=-1, keepdims=True) + EPS) * g_ref[...]
    o_ref[...] = (o * out_scale).astype(o_ref.dtype)


def _attn(proj, lam, subln_g, batch, seq, lambda_init):
    T = proj.shape[0]
    tq = 256
    nq = seq // tq
    return pl.pallas_call(
        functools.partial(_attn_kernel, out_scale=1.0 - lambda_init),
        out_shape=jax.ShapeDtypeStruct((T, DA_HEADS * DA_V_DIM), BF16),
        grid=(batch, DA_HEADS, nq),
        in_specs=[pl.BlockSpec(memory_space=pltpu.SMEM),
                  pl.BlockSpec((tq, LANES), lambda b, h, i: (b * nq + i, COL_DA_Q + h)),
                  pl.BlockSpec((seq, LANES), lambda b, h, i: (b, COL_DA_K + h)),
                  pl.BlockSpec((seq, LANES), lambda b, h, i: (b, COL_DA_V + h)),
                  pl.BlockSpec((1, LANES), lambda b, h, i: (0, 0))],
        out_specs=pl.BlockSpec((tq, LANES), lambda b, h, i: (b * nq + i, h)),
        compiler_params=_params("arbitrary", "arbitrary", "arbitrary"),
        name="attn",
    )(lam, proj, proj, proj, subln_g)


def _lower_bound(tbl, layer):
    e = jnp.exp(tbl - jnp.max(tbl, axis=0, keepdims=True))
    sm = e / jnp.sum(e, axis=0, keepdims=True)
    return jnp.sum(sm[1:layer + 2], axis=0, keepdims=True)


def _hgrn_kernel(q_ref, ff_ref, fb_ref, i_ref, g_ref, tf_ref, tb_ref, ng_ref, o_ref,
                 of_ref, ob_ref, *, layer):
    L = HG_CHUNK
    S = q_ref.shape[0]
    C = S // L
    lb_f = _lower_bound(tf_ref[...], layer)
    lb_b = _lower_bound(tb_ref[...], layer)
    row = lax.broadcasted_iota(jnp.int32, (L, L), 0)
    col = lax.broadcasted_iota(jnp.int32, (L, L), 1)
    tril = row >= col
    triu = row <= col
    tril_b = tril.astype(BF16)
    triu_b = triu.astype(BF16)

    def chunk(c, raw_ref, lb, mask, mask_b, ref_row, last_row, state_t):
        sl = pl.ds(pl.multiple_of(c * L, L), L)
        q = _silu(q_ref[sl, :].astype(F32))
        v = i_ref[sl, :]
        f = lb + (1.0 - lb) * jax.nn.sigmoid(raw_ref[sl, :].astype(F32))
        kk = 1.0 - f
        lf = jnp.log(f)
        hi = lf.astype(BF16)
        lo = (lf - hi.astype(F32)).astype(BF16)
        bs = jnp.dot(mask_b, jnp.concatenate([hi, lo], axis=1), preferred_element_type=F32)
        b = bs[:, :HG_DIM] + bs[:, HG_DIM:]
        r = b[ref_row:ref_row + 1, :]
        bl = b[last_row:last_row + 1, :]
        qi = (q * jnp.exp(b - r)).astype(BF16)
        ki = (kk * jnp.exp(r - b)).astype(BF16)
        a = lax.dot_general(qi, ki, _NT, preferred_element_type=F32)
        a = jnp.where(mask, a, 0.0)
        o = jnp.dot(a.astype(BF16), v, preferred_element_type=F32)
        o += lax.dot_general((q * jnp.exp(b)).astype(BF16), state_t.astype(BF16), _NT,
                             preferred_element_type=F32)
        ku = (kk * jnp.exp(bl - b)).astype(BF16)
        u_t = lax.dot_general(v, ku, _TN, preferred_element_type=F32)
        return o, state_t * jnp.exp(bl) + u_t, sl

    def body(c, carry):
        st_f, st_b = carry
        o_f, st_f, sl_f = chunk(c, ff_ref, lb_f, tril, tril_b, L // 2 - 1, L - 1, st_f)
        of_ref[sl_f, :] = o_f
        o_b, st_b, sl_b = chunk(C - 1 - c, fb_ref, lb_b, triu, triu_b, L // 2, 0, st_b)
        ob_ref[sl_b, :] = o_b
        return st_f, st_b

    zero = jnp.zeros((HG_DIM, HG_DIM), F32)
    lax.fori_loop(0, C, body, (zero, zero))
    o = of_ref[...] + ob_ref[...]
    o = o * lax.rsqrt(jnp.mean(o * o, axis=-1, keepdims=True) + EPS) * ng_ref[...]
    o_ref[...] = (o * _silu(g_ref[...].astype(F32))).astype(o_ref.dtype)


def _hgrn(proj, tbl_f, tbl_b, norm_g, batch, seq, layer):
    T = proj.shape[0]
    nrow = tbl_f.shape[0]
    col = lambda off: pl.BlockSpec((seq, LANES), lambda b, h: (b, off + h))
    tbl = pl.BlockSpec((nrow, LANES), lambda b, h: (0, h))
    return pl.pallas_call(
        functools.partial(_hgrn_kernel, layer=layer),
        out_shape=jax.ShapeDtypeStruct((T, HG_HEADS * HG_DIM), BF16),
        grid=(batch, HG_HEADS),
        in_specs=[col(COL_HG_Q), col(COL_HG_FF), col(COL_HG_FB), col(COL_HG_I), col(COL_HG_G),
                  tbl, tbl, pl.BlockSpec((1, LANES), lambda b, h: (0, 0))],
        out_specs=pl.BlockSpec((seq, LANES), lambda b, h: (b, h)),
        scratch_shapes=[pltpu.VMEM((seq, HG_DIM), F32), pltpu.VMEM((seq, HG_DIM), F32)],
        compiler_params=_params("arbitrary", "arbitrary"),
        name="hgrn",
    )(proj, proj, proj, proj, proj, tbl_f, tbl_b, norm_g)


def _mix_kernel(oa_ref, ob_ref, ga_ref, gb_ref, x_ref, gate_ref, shift_ref, scale_ref, g2_ref,
                wa_ref, wb_ref, wo_ref, wq_ref, x1_ref, h2_ref, qp_ref):
    ya = jnp.dot(oa_ref[...], wa_ref[...], preferred_element_type=F32)
    yb = jnp.dot(ob_ref[...], wb_ref[...], preferred_element_type=F32)
    y = (jax.nn.sigmoid(ga_ref[...].astype(F32)) * ya
         + jax.nn.sigmoid(gb_ref[...].astype(F32)) * yb)
    z = jnp.dot(y.astype(BF16), wo_ref[...], preferred_element_type=F32)
    x1 = x_ref[...] + gate_ref[0] * z
    x1_ref[...] = x1
    h2 = _modulated_norm(x1, g2_ref[...], shift_ref[0], scale_ref[0])
    h2_ref[...] = h2
    qp_ref[...] = jnp.dot(h2.astype(BF16), wq_ref[...], preferred_element_type=F32)


def _mix(o_a, o_b, proj, x2d, gate1, shift2, scale2, g2, wa, wb, wo, wq, seq):
    T, D = x2d.shape
    tm = 256
    gcol = proj.shape[1] // D - 2
    bidx = lambda i: ((i * tm) // seq, 0, 0)
    const = lambda shape: pl.BlockSpec(shape, lambda i: (0, 0), pipeline_mode=pl.Buffered(1))
    row = lambda w: pl.BlockSpec((tm, w), lambda i: (i, 0))
    return pl.pallas_call(
        _mix_kernel,
        out_shape=(jax.ShapeDtypeStruct((T, D), F32), jax.ShapeDtypeStruct((T, D), F32),
                   jax.ShapeDtypeStruct((T, wq.shape[1]), F32)),
        grid=(T // tm,),
        in_specs=[row(o_a.shape[1]), row(o_b.shape[1]),
                  pl.BlockSpec((tm, D), lambda i: (i, gcol)),
                  pl.BlockSpec((tm, D), lambda i: (i, gcol + 1)),
                  row(D),
                  pl.BlockSpec((1, 1, D), bidx), pl.BlockSpec((1, 1, D), bidx),
                  pl.BlockSpec((1, 1, D), bidx),
                  pl.BlockSpec((1, D), lambda i: (0, 0)),
                  const(wa.shape), const(wb.shape), const(wo.shape), const(wq.shape)],
        out_specs=(row(D), row(D), row(wq.shape[1])),
        compiler_params=_params("arbitrary"),
        name="mix",
    )(o_a, o_b, proj, proj, x2d, gate1, shift2, scale2, g2, wa, wb, wo, wq)


def _take_top(s, payload, n):
    rows = s.shape[0]
    iota = lax.broadcasted_iota(jnp.int32, s.shape, 0)
    vals, pays = [], []
    for _ in range(n):
        m = jnp.max(s, axis=0, keepdims=True)
        idx = jnp.min(jnp.where(s == m, iota, rows), axis=0, keepdims=True)
        hit = iota == idx
        vals.append(m)
        pays.append(idx if payload is None else
                    jnp.sum(jnp.where(hit, payload, 0), axis=0, keepdims=True))
        s = jnp.where(hit, -jnp.inf, s)
    return vals, pays


def _topk_kernel(qp_ref, k1_ref, k2_ref, e_ref, g_ref):
    qp = qp_ref[...]
    s1 = lax.dot_general(k1_ref[0], qp[:, :PEER_HALF], _NT, preferred_element_type=F32)
    s2 = lax.dot_general(k2_ref[0], qp[:, PEER_HALF:], _NT, preferred_element_type=F32)
    v1, i1 = _take_top(s1, None, PEER_TOPK)
    v2, i2 = _take_top(s2, None, PEER_TOPK)
    v2c = jnp.concatenate(v2, axis=0)
    i2c = jnp.concatenate(i2, axis=0)
    cand = jnp.concatenate([a + v2c for a in v1], axis=0)
    ids = jnp.concatenate([a * PEER_N_KEYS + i2c for a in i1], axis=0)
    sc, e = _take_top(cand, ids, PEER_TOPK)
    sc = jnp.concatenate(sc, axis=0)
    ex = jnp.exp(sc - sc[0:1, :])
    e_ref[...] = jnp.concatenate(e, axis=0)
    g_ref[...] = ex / jnp.sum(ex, axis=0, keepdims=True)


def _topk(qp, keys1, keys2):
    T = qp.shape[0]
    tt = 256
    return pl.pallas_call(
        _topk_kernel,
        out_shape=(jax.ShapeDtypeStruct((PEER_KK, T), jnp.int32),
                   jax.ShapeDtypeStruct((PEER_KK, T), F32)),
        grid=(T // tt, PEER_HEADS),
        in_specs=[pl.BlockSpec((tt, 2 * PEER_HALF), lambda i, h: (i, h)),
                  pl.BlockSpec((1, PEER_N_KEYS, PEER_HALF), lambda i, h: (h, 0, 0)),
                  pl.BlockSpec((1, PEER_N_KEYS, PEER_HALF), lambda i, h: (h, 0, 0))],
        out_specs=(pl.BlockSpec((PEER_TOPK, tt), lambda i, h: (h, i)),
                   pl.BlockSpec((PEER_TOPK, tt), lambda i, h: (h, i))),
        compiler_params=_params("arbitrary", "arbitrary"),
        name="topk",
    )(qp, keys1, keys2)


PEER_TB = 128
PEER_SLOTS = 4


def _peer_kernel(e_ref, h2_ref, gt_ref, x1_ref, gate_ref, fg_ref, u_hbm, v_hbm, o_ref,
                 ubuf, vbuf, sem_u, sem_v, acc_ref):
    def row_copies(t, slot):
        for k in range(PEER_KK):
            idx = e_ref[t, k]
            yield pltpu.make_async_copy(u_hbm.at[pl.ds(idx, 1)], ubuf.at[slot, pl.ds(k, 1)],
                                        sem_u.at[slot])
            yield pltpu.make_async_copy(v_hbm.at[pl.ds(idx, 1)], vbuf.at[slot, pl.ds(k, 1)],
                                        sem_v.at[slot])

    def issue(t, slot):
        for cp in row_copies(t, slot):
            cp.start()

    def wait(slot):
        pltpu.make_async_copy(u_hbm.at[pl.ds(0, PEER_KK)], ubuf.at[slot], sem_u.at[slot]).wait()
        pltpu.make_async_copy(v_hbm.at[pl.ds(0, PEER_KK)], vbuf.at[slot], sem_v.at[slot]).wait()

    for t0 in range(PEER_SLOTS):
        issue(t0, t0)

    lane = lax.broadcasted_iota(jnp.int32, (PEER_KK, PEER_TB), 1)
    D = h2_ref.shape[1]

    def body(t, carry):
        slot = t % PEER_SLOTS
        wait(slot)
        xt = h2_ref[pl.ds(t, 1), :]
        part = ubuf[slot, :, 0:LANES] * xt[:, 0:LANES]
        for cb in range(1, D // LANES):
            part += ubuf[slot, :, cb * LANES:(cb + 1) * LANES] * xt[:, cb * LANES:(cb + 1) * LANES]
        dots = jnp.sum(part, axis=1, keepdims=True)
        gcol = jnp.sum(jnp.where(lane == t, gt_ref[...], 0.0), axis=1, keepdims=True)
        a = 0.5 * dots * (1.0 + lax.erf(dots * (2.0 ** -0.5))) * gcol
        acc_ref[pl.ds(t, 1), :] = jnp.sum(a * vbuf[slot], axis=0, keepdims=True)

        @pl.when(t + PEER_SLOTS < PEER_TB)
        def _():
            issue(t + PEER_SLOTS, slot)
        return carry

    lax.fori_loop(0, PEER_TB, body, 0)
    x2 = x1_ref[...] + gate_ref[0] * acc_ref[...]
    o_ref[...] = x2 * lax.rsqrt(jnp.mean(x2 * x2, axis=-1, keepdims=True) + EPS) * fg_ref[...]


def _peer(e_tok, h2, g_t, x1, gate2, final_g, u, v, seq):
    T, D = x1.shape
    tb = PEER_TB
    bidx = lambda i: ((i * tb) // seq, 0, 0)
    row = pl.BlockSpec((tb, D), lambda i: (i, 0))
    return pl.pallas_call(
        _peer_kernel,
        out_shape=jax.ShapeDtypeStruct((T, D), F32),
        grid=(T // tb,),
        in_specs=[pl.BlockSpec((tb, PEER_KK), lambda i: (i, 0), memory_space=pltpu.SMEM),
                  row,
                  pl.BlockSpec((PEER_KK, tb), lambda i: (0, i)),
                  row,
                  pl.BlockSpec((1, 1, D), bidx),
                  pl.BlockSpec((1, D), lambda i: (0, 0)),
                  pl.BlockSpec(memory_space=pl.ANY),
                  pl.BlockSpec(memory_space=pl.ANY)],
        out_specs=row,
        scratch_shapes=[pltpu.VMEM((PEER_SLOTS, PEER_KK, D), F32),
                        pltpu.VMEM((PEER_SLOTS, PEER_KK, D), F32),
                        pltpu.SemaphoreType.DMA((PEER_SLOTS,)),
                        pltpu.SemaphoreType.DMA((PEER_SLOTS,)),
                        pltpu.VMEM((tb, D), F32)],
        compiler_params=_params("arbitrary"),
        name="peer",
    )(e_tok, h2, g_t, x1, gate2, final_g, u, v)


def kernel(x, c, positions, w_ada, b_ada, norm1_g, w_in, diff_lq1, diff_lk1, diff_lq2, diff_lk2,
           diff_subln_g, hgrn_lb_fwd, hgrn_lb_bwd, hgrn_norm_g, w_branch_attn, w_branch_hgrn,
           w_out, norm2_g, peer_wq, peer_keys1, peer_keys2, peer_u, peer_v, final_norm_g):
    B, S, D = x.shape
    T = B * S
    depth = w_ada.shape[0]
    assert depth == 1, "the fused final-norm epilogue assumes a single layer"
    rc, rs1, rs2 = _rope_lane_tables(positions)
    x2d = x.reshape(T, D)
    l = 0
    ada = _ada(c, w_ada[l], b_ada[l])
    shift1, scale1, gate1, shift2, scale2, gate2 = (
        ada[:, i * D:(i + 1) * D].reshape(B, 1, D) for i in range(6))
    proj = _proj(x2d, shift1, scale1, norm1_g[l].reshape(1, D), w_in[l].astype(BF16),
                 rc, rs1, rs2, S)
    lambda_init = 0.8 - 0.6 * float(np.exp(-0.3 * l))
    lam = (jnp.exp(jnp.sum(diff_lq1[l] * diff_lk1[l])) - jnp.exp(jnp.sum(diff_lq2[l] * diff_lk2[l]))
           + lambda_init).reshape(1).astype(F32)
    o_a = _attn(proj, lam, diff_subln_g[l].reshape(1, DA_V_DIM), B, S, lambda_init)
    o_b = _hgrn(proj, hgrn_lb_fwd, hgrn_lb_bwd, hgrn_norm_g[l].reshape(1, HG_DIM), B, S, l)
    x1, h2, qp = _mix(o_a, o_b, proj, x2d, gate1, shift2, scale2, norm2_g[l].reshape(1, D),
                      w_branch_attn[l].astype(BF16), w_branch_hgrn[l].astype(BF16),
                      w_out[l].astype(BF16), peer_wq[l].astype(BF16), S)
    e_t, g_t = _topk(qp, peer_keys1[l], peer_keys2[l])
    out = _peer(e_t.T, h2, g_t, x1, gate2, final_norm_g.reshape(1, D), peer_u[l], peer_v[l], S)
    return out.reshape(B, S, D)
```

```python
import functools

import numpy as np
import jax
import jax.numpy as jnp
from jax import lax
from jax.experimental import pallas as pl
from jax.experimental.pallas import tpu as pltpu

F32 = jnp.float32
BF16 = jnp.bfloat16

EPS = 1e-6
DA_HEADS = 8
DA_QK_DIM = 64
DA_V_DIM = 128
ROPE_DIM = DA_QK_DIM // 4
ROPE_HALF = ROPE_DIM // 2
ROPE_THETA = 500000.0
HG_HEADS = 8
HG_DIM = 128
HG_CHUNK = 64
PEER_HEADS = 8
PEER_N_KEYS = 128
PEER_HALF = 128
PEER_TOPK = 16
PEER_KK = PEER_HEADS * PEER_TOPK

LANES = 128
VMEM_LIMIT = 56 * 1024 * 1024

COL_DA_Q, COL_DA_K, COL_DA_V = 0, 8, 16
COL_HG_Q, COL_HG_FF, COL_HG_FB, COL_HG_I, COL_HG_G = 24, 32, 40, 48, 56

_NT = (((1,), (1,)), ((), ()))
_TN = (((0,), (0,)), ((), ()))


def _params(*sem):
    return pltpu.CompilerParams(dimension_semantics=sem, vmem_limit_bytes=VMEM_LIMIT)


def _silu(x):
    return x * jax.nn.sigmoid(x)


def _ada_kernel(c_ref, w_ref, b_ref, o_ref):
    s = _silu(c_ref[...])
    s_hi = s.astype(BF16)
    s_lo = (s - s_hi.astype(F32)).astype(BF16)
    w = w_ref[...]
    w_hi = w.astype(BF16)
    w_lo = (w - w_hi.astype(F32)).astype(BF16)
    acc = jnp.dot(s_hi, w_hi, preferred_element_type=F32)
    acc += jnp.dot(s_lo, w_hi, preferred_element_type=F32)
    acc += jnp.dot(s_hi, w_lo, preferred_element_type=F32)
    o_ref[...] = acc + b_ref[...]


def _ada(c, w, b):
    B, D = c.shape
    N = w.shape[1]
    tn = 1024
    return pl.pallas_call(
        _ada_kernel,
        out_shape=jax.ShapeDtypeStruct((B, N), F32),
        grid=(N // tn,),
        in_specs=[pl.BlockSpec((B, D), lambda j: (0, 0)),
                  pl.BlockSpec((D, tn), lambda j: (0, j)),
                  pl.BlockSpec((1, tn), lambda j: (0, j))],
        out_specs=pl.BlockSpec((B, tn), lambda j: (0, j)),
        compiler_params=_params("arbitrary"),
        name="ada",
    )(c, w, b.reshape(1, N))


def _modulated_norm(xf, g, shift, scale):
    y = xf * lax.rsqrt(jnp.mean(xf * xf, axis=-1, keepdims=True) + EPS) * g
    return y * (1.0 + scale) + shift


def _proj_kernel(x_ref, shift_ref, scale_ref, g_ref, w_ref, rc_ref, rs1_ref, rs2_ref, o_ref, h_ref):
    j = pl.program_id(1)

    @pl.when(j == 0)
    def _():
        h_ref[...] = _modulated_norm(x_ref[...], g_ref[...], shift_ref[0], scale_ref[0]).astype(BF16)

    acc = jnp.dot(h_ref[...], w_ref[...], preferred_element_type=F32)
    tn = acc.shape[1]

    def store_rotated(mult):
        rc, rs1, rs2 = rc_ref[...], rs1_ref[...], rs2_ref[...]
        for cb in range(tn // LANES):
            t = acc[:, cb * LANES:(cb + 1) * LANES]
            r = (t * rc + pltpu.roll(t, ROPE_HALF, axis=1) * rs1
                 + pltpu.roll(t, LANES - ROPE_HALF, axis=1) * rs2)
            o_ref[:, cb * LANES:(cb + 1) * LANES] = (r * mult).astype(o_ref.dtype)

    @pl.when(j == 0)
    def _():
        store_rotated(DA_QK_DIM ** -0.5)

    @pl.when(j == 1)
    def _():
        store_rotated(1.0)

    @pl.when(j >= 2)
    def _():
        o_ref[...] = acc.astype(o_ref.dtype)


def _proj(x2d, shift, scale, g, w_bf16, rc, rs1, rs2, seq):
    T, D = x2d.shape
    N = w_bf16.shape[1]
    tm, tn = min(1024, seq), 1024
    assert seq % tm == 0 and N % tn == 0 and tn == DA_HEADS * 2 * DA_QK_DIM
    bidx = lambda i, j: ((i * tm) // seq, 0, 0)
    return pl.pallas_call(
        _proj_kernel,
        out_shape=jax.ShapeDtypeStruct((T, N), BF16),
        grid=(T // tm, N // tn),
        in_specs=[pl.BlockSpec((tm, D), lambda i, j: (i, 0)),
                  pl.BlockSpec((1, 1, D), bidx),
                  pl.BlockSpec((1, 1, D), bidx),
                  pl.BlockSpec((1, D), lambda i, j: (0, 0)),
                  pl.BlockSpec((D, tn), lambda i, j: (0, j)),
                  pl.BlockSpec((tm, LANES), lambda i, j: (i, 0)),
                  pl.BlockSpec((tm, LANES), lambda i, j: (i, 0)),
                  pl.BlockSpec((tm, LANES), lambda i, j: (i, 0))],
        out_specs=pl.BlockSpec((tm, tn), lambda i, j: (i, j)),
        scratch_shapes=[pltpu.VMEM((tm, D), BF16)],
        compiler_params=_params("arbitrary", "arbitrary"),
        name="proj",
    )(x2d, shift, scale, g, w_bf16, rc, rs1, rs2)


def _rope_lane_tables(positions):
    B, S = positions.shape
    T = B * S
    inv = ROPE_THETA ** (-jnp.arange(0, ROPE_DIM, 2, dtype=F32) / ROPE_DIM)
    ang = positions.astype(F32).reshape(T, 1) * inv
    cos, sin = jnp.cos(ang), jnp.sin(ang)
    one = jnp.ones((T, DA_QK_DIM - ROPE_DIM), F32)
    zero8 = jnp.zeros((T, ROPE_HALF), F32)
    zero48 = jnp.zeros((T, DA_QK_DIM - ROPE_DIM), F32)
    rc = jnp.concatenate([cos, cos, one], axis=1)
    rs1 = jnp.concatenate([zero8, sin, zero48], axis=1)
    rs2 = jnp.concatenate([-sin, zero8, zero48], axis=1)
    rep = LANES // DA_QK_DIM
    return jnp.tile(rc, (1, rep)), jnp.tile(rs1, (1, rep)), jnp.tile(rs2, (1, rep))


def _attn_kernel(lam_ref, q_ref, k_ref, v_ref, g_ref, o_ref, *, out_scale):
    q = q_ref[...]
    k = k_ref[...]
    v = v_ref[...]
    lane = lax.broadcasted_iota(jnp.int32, q.shape, 1)
    zero = jnp.zeros_like(q)

    def one_map(qm):
        s = lax.dot_general(qm, k, _NT, preferred_element_type=F32)
        e = jnp.exp(s - jnp.max(s, axis=-1, keepdims=True))
        l = jnp.sum(e, axis=-1, keepdims=True)
        return jnp.dot(e.astype(v.dtype), v, preferred_element_type=F32) / l

    o1 = one_map(jnp.where(lane < DA_QK_DIM, q, zero))
    o2 = one_map(jnp.where(lane >= DA_QK_DIM, q, zero))
    o = o1 - lam_ref[0] * o2
    o = o * lax.rsqrt(jnp.mean(o * o, axis=-1, keepdims=True) + EPS) * g_ref[...]
    o_ref[...] = (o * out_scale).astype(o_ref.dtype)


def _attn(proj, lam, subln_g, batch, seq, lambda_init):
    T = proj.shape[0]
    tq = 256
    nq = seq // tq
    return pl.pallas_call(
        functools.partial(_attn_kernel, out_scale=1.0 - lambda_init),
        out_shape=jax.ShapeDtypeStruct((T, DA_HEADS * DA_V_DIM), BF16),
        grid=(batch, DA_HEADS, nq),
        in_specs=[pl.BlockSpec(memory_space=pltpu.SMEM),
                  pl.BlockSpec((tq, LANES), lambda b, h, i: (b * nq + i, COL_DA_Q + h)),
                  pl.BlockSpec((seq, LANES), lambda b, h, i: (b, COL_DA_K + h)),
                  pl.BlockSpec((seq, LANES), lambda b, h, i: (b, COL_DA_V + h)),
                  pl.BlockSpec((1, LANES), lambda b, h, i: (0, 0))],
        out_specs=pl.BlockSpec((tq, LANES), lambda b, h, i: (b * nq + i, h)),
        compiler_params=_params("arbitrary", "arbitrary", "arbitrary"),
        name="attn",
    )(lam, proj, proj, proj, subln_g)


def _lower_bound(tbl, layer):
    e = jnp.exp(tbl - jnp.max(tbl, axis=0, keepdims=True))
    sm = e / jnp.sum(e, axis=0, keepdims=True)
    return jnp.sum(sm[1:layer + 2], axis=0, keepdims=True)


def _hgrn_kernel(q_ref, ff_ref, fb_ref, i_ref, g_ref, tf_ref, tb_ref, ng_ref, o_ref,
                 of_ref, ob_ref, *, layer):
    L = HG_CHUNK
    S = q_ref.shape[0]
    C = S // L
    lb_f = _lower_bound(tf_ref[...], layer)
    lb_b = _lower_bound(tb_ref[...], layer)
    row = lax.broadcasted_iota(jnp.int32, (L, L), 0)
    col = lax.broadcasted_iota(jnp.int32, (L, L), 1)
    tril = row >= col
    triu = row <= col
    tril_b = tril.astype(BF16)
    triu_b = triu.astype(BF16)

    def chunk(c, raw_ref, lb, mask, mask_b, ref_row, last_row, state_t):
        sl = pl.ds(pl.multiple_of(c * L, L), L)
        q = _silu(q_ref[sl, :].astype(F32))
        v = i_ref[sl, :]
        f = lb + (1.0 - lb) * jax.nn.sigmoid(raw_ref[sl, :].astype(F32))
        kk = 1.0 - f
        lf = jnp.log(f)
        hi = lf.astype(BF16)
        lo = (lf - hi.astype(F32)).astype(BF16)
        bs = jnp.dot(mask_b, jnp.concatenate([hi, lo], axis=1), preferred_element_type=F32)
        b = bs[:, :HG_DIM] + bs[:, HG_DIM:]
        r = b[ref_row:ref_row + 1, :]
        bl = b[last_row:last_row + 1, :]
        qi = (q * jnp.exp(b - r)).astype(BF16)
        ki = (kk * jnp.exp(r - b)).astype(BF16)
        a = lax.dot_general(qi, ki, _NT, preferred_element_type=F32)
        a = jnp.where(mask, a, 0.0)
        o = jnp.dot(a.astype(BF16), v, preferred_element_type=F32)
        o += lax.dot_general((q * jnp.exp(b)).astype(BF16), state_t.astype(BF16), _NT,
                             preferred_element_type=F32)
        ku = (kk * jnp.exp(bl - b)).astype(BF16)
        u_t = lax.dot_general(v, ku, _TN, preferred_element_type=F32)
        return o, state_t * jnp.exp(bl) + u_t, sl

    def body(c, carry):
        st_f, st_b = carry
        o_f, st_f, sl_f = chunk(c, ff_ref, lb_f, tril, tril_b, L // 2 - 1, L - 1, st_f)
        of_ref[sl_f, :] = o_f
        o_b, st_b, sl_b = chunk(C - 1 - c, fb_ref, lb_b, triu, triu_b, L // 2, 0, st_b)
        ob_ref[sl_b, :] = o_b
        return st_f, st_b

    zero = jnp.zeros((HG_DIM, HG_DIM), F32)
    lax.fori_loop(0, C, body, (zero, zero), unroll=2)
    o = of_ref[...] + ob_ref[...]
    o = o * lax.rsqrt(jnp.mean(o * o, axis=-1, keepdims=True) + EPS) * ng_ref[...]
    o_ref[...] = (o * _silu(g_ref[...].astype(F32))).astype(o_ref.dtype)


def _hgrn(proj, tbl_f, tbl_b, norm_g, batch, seq, layer):
    T = proj.shape[0]
    nrow = tbl_f.shape[0]
    col = lambda off: pl.BlockSpec((seq, LANES), lambda b, h: (b, off + h))
    tbl = pl.BlockSpec((nrow, LANES), lambda b, h: (0, h))
    return pl.pallas_call(
        functools.partial(_hgrn_kernel, layer=layer),
        out_shape=jax.ShapeDtypeStruct((T, HG_HEADS * HG_DIM), BF16),
        grid=(batch, HG_HEADS),
        in_specs=[col(COL_HG_Q), col(COL_HG_FF), col(COL_HG_FB), col(COL_HG_I), col(COL_HG_G),
                  tbl, tbl, pl.BlockSpec((1, LANES), lambda b, h: (0, 0))],
        out_specs=pl.BlockSpec((seq, LANES), lambda b, h: (b, h)),
        scratch_shapes=[pltpu.VMEM((seq, HG_DIM), F32), pltpu.VMEM((seq, HG_DIM), F32)],
        compiler_params=_params("arbitrary", "arbitrary"),
        name="hgrn",
    )(proj, proj, proj, proj, proj, tbl_f, tbl_b, norm_g)


def _mix_kernel(oa_ref, ob_ref, ga_ref, gb_ref, x_ref, gate_ref, shift_ref, scale_ref, g2_ref,
                wa_ref, wb_ref, wo_ref, wq_ref, x1_ref, h2_ref, qp_ref):
    ya = jnp.dot(oa_ref[...], wa_ref[...], preferred_element_type=F32)
    yb = jnp.dot(ob_ref[...], wb_ref[...], preferred_element_type=F32)
    y = (jax.nn.sigmoid(ga_ref[...].astype(F32)) * ya
         + jax.nn.sigmoid(gb_ref[...].astype(F32)) * yb)
    z = jnp.dot(y.astype(BF16), wo_ref[...], preferred_element_type=F32)
    x1 = x_ref[...] + gate_ref[0] * z
    x1_ref[...] = x1
    h2 = _modulated_norm(x1, g2_ref[...], shift_ref[0], scale_ref[0])
    h2_ref[...] = h2
    qp_ref[...] = jnp.dot(h2.astype(BF16), wq_ref[...], preferred_element_type=F32)


def _mix(o_a, o_b, proj, x2d, gate1, shift2, scale2, g2, wa, wb, wo, wq, seq):
    T, D = x2d.shape
    tm = 256
    gcol = proj.shape[1] // D - 2
    bidx = lambda i: ((i * tm) // seq, 0, 0)
    const = lambda shape: pl.BlockSpec(shape, lambda i: (0, 0), pipeline_mode=pl.Buffered(1))
    row = lambda w: pl.BlockSpec((tm, w), lambda i: (i, 0))
    return pl.pallas_call(
        _mix_kernel,
        out_shape=(jax.ShapeDtypeStruct((T, D), F32), jax.ShapeDtypeStruct((T, D), F32),
                   jax.ShapeDtypeStruct((T, wq.shape[1]), F32)),
        grid=(T // tm,),
        in_specs=[row(o_a.shape[1]), row(o_b.shape[1]),
                  pl.BlockSpec((tm, D), lambda i: (i, gcol)),
                  pl.BlockSpec((tm, D), lambda i: (i, gcol + 1)),
                  row(D),
                  pl.BlockSpec((1, 1, D), bidx), pl.BlockSpec((1, 1, D), bidx),
                  pl.BlockSpec((1, 1, D), bidx),
                  pl.BlockSpec((1, D), lambda i: (0, 0)),
                  const(wa.shape), const(wb.shape), const(wo.shape), const(wq.shape)],
        out_specs=(row(D), row(D), row(wq.shape[1])),
        compiler_params=_params("arbitrary"),
        name="mix",
    )(o_a, o_b, proj, proj, x2d, gate1, shift2, scale2, g2, wa, wb, wo, wq)


_BIG = 1e9


def _take_top(s, order, payload, n):
    vals, picked = [], []
    for _ in range(n):
        m = jnp.max(s, axis=0, keepdims=True)
        first = jnp.min(jnp.where(s == m, order, _BIG), axis=0, keepdims=True)
        hit = order == first
        vals.append(m)
        picked.append(first if payload is None else
                      jnp.sum(jnp.where(hit, payload, 0.0), axis=0, keepdims=True))
        s = jnp.where(hit, -jnp.inf, s)
    return vals, picked


def _staircase(v1, i1, v2, i2):
    k = PEER_TOPK
    W = v1[0].shape[1]
    row = lax.broadcasted_iota(jnp.int32, (8, W), 0)
    rowf = row.astype(F32)
    v1c, i1c = jnp.concatenate(v1, axis=0), jnp.concatenate(i1, axis=0)
    v2c, i2c = jnp.concatenate(v2, axis=0), jnp.concatenate(i2, axis=0)
    v2lo, i2lo = v2c[0:8], i2c[0:8]

    def sel3(x5, x6, x7):
        return jnp.where(row < 2, x5, jnp.where(row < 4, x6, x7))

    def shifted(x, n):
        return pltpu.roll(x, n, axis=0)

    pieces = [
        (v1[0], i1[0], v2lo, i2lo, rowf, None),
        (v1[0], i1[0], v2c[8:16], i2c[8:16], 8.0 + rowf, None),
        (v1[1], i1[1], v2lo, i2lo, k + rowf, None),
        (v1[2], i1[2], v2lo, i2lo, 2 * k + rowf, row < 5),
        (jnp.where(row < 4, v1[3], v1[4]), jnp.where(row < 4, i1[3], i1[4]),
         jnp.where(row < 4, v2lo, shifted(v2lo, 4)), jnp.where(row < 4, i2lo, shifted(i2lo, 4)),
         jnp.where(row < 4, 3 * k + rowf, 4 * k - 4 + rowf), row < 7),
        (sel3(v1[5], v1[6], v1[7]), sel3(i1[5], i1[6], i1[7]),
         sel3(v2lo, shifted(v2lo, 2), shifted(v2lo, 4)), sel3(i2lo, shifted(i2lo, 2), shifted(i2lo, 4)),
         sel3(5 * k + rowf, 6 * k - 2 + rowf, 7 * k - 4 + rowf), row < 6),
        (v1c[8:16], i1c[8:16], v2[0], i2[0], (8.0 + rowf) * k, None),
    ]
    cand, order, ids = [], [], []
    for va, ia, vb, ib, flat, valid in pieces:
        c = va + vb
        if valid is not None:
            c = jnp.where(valid, c, -jnp.inf)
            flat = jnp.where(valid, flat, _BIG)
        cand.append(c)
        order.append(flat)
        ids.append(ia * PEER_N_KEYS + ib)
    return (jnp.concatenate(cand, axis=0), jnp.concatenate(order, axis=0),
            jnp.concatenate(ids, axis=0))


def _topk_kernel(qp_ref, k1_ref, k2_ref, e_ref, g_ref):
    qp = qp_ref[...]
    s1 = lax.dot_general(k1_ref[0], qp[:, :PEER_HALF], _NT, preferred_element_type=F32)
    s2 = lax.dot_general(k2_ref[0], qp[:, PEER_HALF:], _NT, preferred_element_type=F32)
    key_rank = lax.broadcasted_iota(jnp.int32, s1.shape, 0).astype(F32)
    v1, i1 = _take_top(s1, key_rank, None, PEER_TOPK)
    v2, i2 = _take_top(s2, key_rank, None, PEER_TOPK)
    cand, order, ids = _staircase(v1, i1, v2, i2)
    sc, e = _take_top(cand, order, ids, PEER_TOPK)
    sc = jnp.concatenate(sc, axis=0)
    ex = jnp.exp(sc - sc[0:1, :])
    e_ref[...] = jnp.concatenate(e, axis=0).astype(jnp.int32)
    g_ref[...] = ex / jnp.sum(ex, axis=0, keepdims=True)


def _topk(qp, keys1, keys2):
    T = qp.shape[0]
    tt = 256
    return pl.pallas_call(
        _topk_kernel,
        out_shape=(jax.ShapeDtypeStruct((PEER_KK, T), jnp.int32),
                   jax.ShapeDtypeStruct((PEER_KK, T), F32)),
        grid=(T // tt, PEER_HEADS),
        in_specs=[pl.BlockSpec((tt, 2 * PEER_HALF), lambda i, h: (i, h)),
                  pl.BlockSpec((1, PEER_N_KEYS, PEER_HALF), lambda i, h: (h, 0, 0)),
                  pl.BlockSpec((1, PEER_N_KEYS, PEER_HALF), lambda i, h: (h, 0, 0))],
        out_specs=(pl.BlockSpec((PEER_TOPK, tt), lambda i, h: (h, i)),
                   pl.BlockSpec((PEER_TOPK, tt), lambda i, h: (h, i))),
        compiler_params=_params("arbitrary", "arbitrary"),
        name="topk",
    )(qp, keys1, keys2)


PEER_TB = 128
PEER_SLOTS = 8


def _pack_expert_tables(u, v):
    ub = lax.bitcast_convert_type(u.astype(BF16), jnp.uint16).astype(jnp.uint32)
    vb = lax.bitcast_convert_type(v.astype(BF16), jnp.uint16).astype(jnp.uint32)
    return ((vb << 16) | ub).reshape(u.shape[0], 1, u.shape[1])


def _peer_kernel(e_ref, h2_ref, gt_ref, x1_ref, gate_ref, fg_ref, w_hbm, o_ref,
                 wbuf, sem, acc_ref):
    D = h2_ref.shape[1]
    col_blocks = [slice(cb * LANES, (cb + 1) * LANES) for cb in range(D // LANES)]

    def issue(t, slot):
        for k in range(PEER_KK):
            pltpu.make_async_copy(w_hbm.at[e_ref[t, k]], wbuf.at[slot, pl.ds(k, 1)],
                                  sem.at[slot]).start(priority=k % 2)

    def wait(slot):
        pltpu.make_async_copy(w_hbm.at[pl.ds(0, PEER_KK), 0], wbuf.at[slot], sem.at[slot]).wait()

    lane = lax.broadcasted_iota(jnp.int32, (PEER_KK, PEER_TB), 1)

    def expert_mlp(t, slot):
        xt = h2_ref[pl.ds(t, 1), :]
        part = jnp.zeros((PEER_KK, LANES), F32)
        for cols in col_blocks:
            part += lax.bitcast_convert_type(wbuf[slot, :, cols] << 16, F32) * xt[:, cols]
        dots = jnp.sum(part, axis=1, keepdims=True)
        gcol = jnp.sum(jnp.where(lane == t, gt_ref[...], 0.0), axis=1, keepdims=True)
        a = 0.5 * dots * (1.0 + lax.erf(dots * (2.0 ** -0.5))) * gcol
        a_b = jnp.broadcast_to(a, (PEER_KK, LANES))
        pieces = []
        for cols in col_blocks:
            vrows = lax.bitcast_convert_type(wbuf[slot, :, cols] & jnp.uint32(0xFFFF0000), F32)
            pieces.append(jnp.sum(a_b * vrows, axis=0, keepdims=True))
        return jnp.concatenate(pieces, axis=1)

    for slot in range(PEER_SLOTS):
        issue(slot, slot)

    def steady(g, carry):
        for slot in range(PEER_SLOTS):
            t = g * PEER_SLOTS + slot
            wait(slot)
            row = expert_mlp(t, slot)
            issue(t + PEER_SLOTS, slot)
            acc_ref[pl.ds(t, 1), :] = row
        return carry

    n_groups = PEER_TB // PEER_SLOTS
    lax.fori_loop(0, n_groups - 1, steady, 0)
    for slot in range(PEER_SLOTS):
        t = (n_groups - 1) * PEER_SLOTS + slot
        wait(slot)
        acc_ref[pl.ds(t, 1), :] = expert_mlp(t, slot)
    x2 = x1_ref[...] + gate_ref[0] * acc_ref[...]
    o_ref[...] = x2 * lax.rsqrt(jnp.mean(x2 * x2, axis=-1, keepdims=True) + EPS) * fg_ref[...]


def _peer(e_tok, h2, g_t, x1, gate2, final_g, w_packed, seq):
    T, D = x1.shape
    tb = PEER_TB
    bidx = lambda i: ((i * tb) // seq, 0, 0)
    row = pl.BlockSpec((tb, D), lambda i: (i, 0))
    return pl.pallas_call(
        _peer_kernel,
        out_shape=jax.ShapeDtypeStruct((T, D), F32),
        grid=(T // tb,),
        in_specs=[pl.BlockSpec((tb, PEER_KK), lambda i: (i, 0), memory_space=pltpu.SMEM),
                  row,
                  pl.BlockSpec((PEER_KK, tb), lambda i: (0, i)),
                  row,
                  pl.BlockSpec((1, 1, D), bidx),
                  pl.BlockSpec((1, D), lambda i: (0, 0)),
                  pl.BlockSpec(memory_space=pl.ANY)],
        out_specs=row,
        scratch_shapes=[pltpu.VMEM((PEER_SLOTS, PEER_KK, D), jnp.uint32),
                        pltpu.SemaphoreType.DMA((PEER_SLOTS,)),
                        pltpu.VMEM((tb, D), F32)],
        compiler_params=_params("arbitrary"),
        name="peer",
    )(e_tok, h2, g_t, x1, gate2, final_g, w_packed)


def kernel(x, c, positions, w_ada, b_ada, norm1_g, w_in, diff_lq1, diff_lk1, diff_lq2, diff_lk2,
           diff_subln_g, hgrn_lb_fwd, hgrn_lb_bwd, hgrn_norm_g, w_branch_attn, w_branch_hgrn,
           w_out, norm2_g, peer_wq, peer_keys1, peer_keys2, peer_u, peer_v, final_norm_g):
    B, S, D = x.shape
    T = B * S
    depth = w_ada.shape[0]
    assert depth == 1, "the fused final-norm epilogue assumes a single layer"
    rc, rs1, rs2 = _rope_lane_tables(positions)
    x2d = x.reshape(T, D)
    l = 0
    ada = _ada(c, w_ada[l], b_ada[l])
    shift1, scale1, gate1, shift2, scale2, gate2 = (
        ada[:, i * D:(i + 1) * D].reshape(B, 1, D) for i in range(6))
    proj = _proj(x2d, shift1, scale1, norm1_g[l].reshape(1, D), w_in[l].astype(BF16),
                 rc, rs1, rs2, S)
    lambda_init = 0.8 - 0.6 * float(np.exp(-0.3 * l))
    lam = (jnp.exp(jnp.sum(diff_lq1[l] * diff_lk1[l])) - jnp.exp(jnp.sum(diff_lq2[l] * diff_lk2[l]))
           + lambda_init).reshape(1).astype(F32)
    o_a = _attn(proj, lam, diff_subln_g[l].reshape(1, DA_V_DIM), B, S, lambda_init)
    o_b = _hgrn(proj, hgrn_lb_fwd, hgrn_lb_bwd, hgrn_norm_g[l].reshape(1, HG_DIM), B, S, l)
    x1, h2, qp = _mix(o_a, o_b, proj, x2d, gate1, shift2, scale2, norm2_g[l].reshape(1, D),
                      w_branch_attn[l].astype(BF16), w_branch_hgrn[l].astype(BF16),
                      w_out[l].astype(BF16), peer_wq[l].astype(BF16), S)
    e_t, g_t = _topk(qp, peer_keys1[l], peer_keys2[l])
    out = _peer(e_t.T, h2, g_t, x1, gate2, final_norm_g.reshape(1, D),
                _pack_expert_tables(peer_u[l], peer_v[l]), S)
    return out.reshape(B, S, D)
```

```python
import functools

import numpy as np
import jax
import jax.numpy as jnp
from jax import lax
from jax.experimental import pallas as pl
from jax.experimental.pallas import tpu as pltpu

F32 = jnp.float32
BF16 = jnp.bfloat16

EPS = 1e-6
DA_HEADS = 8
DA_QK_DIM = 64
DA_V_DIM = 128
ROPE_DIM = DA_QK_DIM // 4
ROPE_HALF = ROPE_DIM // 2
ROPE_THETA = 500000.0
HG_HEADS = 8
HG_DIM = 128
HG_CHUNK = 64
PEER_HEADS = 8
PEER_N_KEYS = 128
PEER_HALF = 128
PEER_TOPK = 16
PEER_KK = PEER_HEADS * PEER_TOPK

LANES = 128
VMEM_LIMIT = 56 * 1024 * 1024

COL_DA_Q, COL_DA_K, COL_DA_V = 0, 8, 16
COL_HG_Q, COL_HG_FF, COL_HG_FB, COL_HG_I, COL_HG_G = 24, 32, 40, 48, 56

_NT = (((1,), (1,)), ((), ()))
_TN = (((0,), (0,)), ((), ()))


def _params(*sem):
    return pltpu.CompilerParams(dimension_semantics=sem, vmem_limit_bytes=VMEM_LIMIT)


def _silu(x):
    return x * jax.nn.sigmoid(x)


def _ada_kernel(c_ref, w_ref, b_ref, o_ref):
    s = _silu(c_ref[...])
    s_hi = s.astype(BF16)
    s_lo = (s - s_hi.astype(F32)).astype(BF16)
    w = w_ref[...]
    w_hi = w.astype(BF16)
    w_lo = (w - w_hi.astype(F32)).astype(BF16)
    acc = jnp.dot(s_hi, w_hi, preferred_element_type=F32)
    acc += jnp.dot(s_lo, w_hi, preferred_element_type=F32)
    acc += jnp.dot(s_hi, w_lo, preferred_element_type=F32)
    o_ref[...] = acc + b_ref[...]


def _ada(c, w, b):
    B, D = c.shape
    N = w.shape[1]
    tn = 1024
    return pl.pallas_call(
        _ada_kernel,
        out_shape=jax.ShapeDtypeStruct((B, N), F32),
        grid=(N // tn,),
        in_specs=[pl.BlockSpec((B, D), lambda j: (0, 0)),
                  pl.BlockSpec((D, tn), lambda j: (0, j)),
                  pl.BlockSpec((1, tn), lambda j: (0, j))],
        out_specs=pl.BlockSpec((B, tn), lambda j: (0, j)),
        compiler_params=_params("arbitrary"),
        name="ada",
    )(c, w, b.reshape(1, N))


def _modulated_norm(xf, g, shift, scale):
    y = xf * lax.rsqrt(jnp.mean(xf * xf, axis=-1, keepdims=True) + EPS) * g
    return y * (1.0 + scale) + shift


def _proj_kernel(x_ref, shift_ref, scale_ref, g_ref, w_ref, rc_ref, rs1_ref, rs2_ref, o_ref, h_ref):
    j = pl.program_id(1)

    @pl.when(j == 0)
    def _():
        h_ref[...] = _modulated_norm(x_ref[...], g_ref[...], shift_ref[0], scale_ref[0]).astype(BF16)

    acc = jnp.dot(h_ref[...], w_ref[...], preferred_element_type=F32)
    tn = acc.shape[1]

    def store_rotated(mult):
        rc, rs1, rs2 = rc_ref[...], rs1_ref[...], rs2_ref[...]
        for cb in range(tn // LANES):
            t = acc[:, cb * LANES:(cb + 1) * LANES]
            r = (t * rc + pltpu.roll(t, ROPE_HALF, axis=1) * rs1
                 + pltpu.roll(t, LANES - ROPE_HALF, axis=1) * rs2)
            o_ref[:, cb * LANES:(cb + 1) * LANES] = (r * mult).astype(o_ref.dtype)

    @pl.when(j == 0)
    def _():
        store_rotated(DA_QK_DIM ** -0.5)

    @pl.when(j == 1)
    def _():
        store_rotated(1.0)

    @pl.when(j >= 2)
    def _():
        o_ref[...] = acc.astype(o_ref.dtype)


def _proj(x2d, shift, scale, g, w_bf16, rc, rs1, rs2, seq):
    T, D = x2d.shape
    N = w_bf16.shape[1]
    tm, tn = min(1024, seq), 1024
    assert seq % tm == 0 and N % tn == 0 and tn == DA_HEADS * 2 * DA_QK_DIM
    bidx = lambda i, j: ((i * tm) // seq, 0, 0)
    return pl.pallas_call(
        _proj_kernel,
        out_shape=jax.ShapeDtypeStruct((T, N), BF16),
        grid=(T // tm, N // tn),
        in_specs=[pl.BlockSpec((tm, D), lambda i, j: (i, 0)),
                  pl.BlockSpec((1, 1, D), bidx),
                  pl.BlockSpec((1, 1, D), bidx),
                  pl.BlockSpec((1, D), lambda i, j: (0, 0)),
                  pl.BlockSpec((D, tn), lambda i, j: (0, j)),
                  pl.BlockSpec((tm, LANES), lambda i, j: (i, 0)),
                  pl.BlockSpec((tm, LANES), lambda i, j: (i, 0)),
                  pl.BlockSpec((tm, LANES), lambda i, j: (i, 0))],
        out_specs=pl.BlockSpec((tm, tn), lambda i, j: (i, j)),
        scratch_shapes=[pltpu.VMEM((tm, D), BF16)],
        compiler_params=_params("arbitrary", "arbitrary"),
        name="proj",
    )(x2d, shift, scale, g, w_bf16, rc, rs1, rs2)


def _rope_lane_tables(positions):
    B, S = positions.shape
    T = B * S
    inv = ROPE_THETA ** (-jnp.arange(0, ROPE_DIM, 2, dtype=F32) / ROPE_DIM)
    ang = positions.astype(F32).reshape(T, 1) * inv
    cos, sin = jnp.cos(ang), jnp.sin(ang)
    one = jnp.ones((T, DA_QK_DIM - ROPE_DIM), F32)
    zero8 = jnp.zeros((T, ROPE_HALF), F32)
    zero48 = jnp.zeros((T, DA_QK_DIM - ROPE_DIM), F32)
    rc = jnp.concatenate([cos, cos, one], axis=1)
    rs1 = jnp.concatenate([zero8, sin, zero48], axis=1)
    rs2 = jnp.concatenate([-sin, zero8, zero48], axis=1)
    rep = LANES // DA_QK_DIM
    return jnp.tile(rc, (1, rep)), jnp.tile(rs1, (1, rep)), jnp.tile(rs2, (1, rep))


def _attn_kernel(lam_ref, q_ref, k_ref, v_ref, g_ref, o_ref, *, out_scale):
    q = q_ref[...]
    k = k_ref[...]
    v = v_ref[...]
    lane = lax.broadcasted_iota(jnp.int32, q.shape, 1)
    zero = jnp.zeros_like(q)

    def one_map(qm):
        s = lax.dot_general(qm, k, _NT, preferred_element_type=F32)
        e = jnp.exp(s - jnp.max(s, axis=-1, keepdims=True))
        l = jnp.sum(e, axis=-1, keepdims=True)
        return jnp.dot(e.astype(v.dtype), v, preferred_element_type=F32) / l

    o1 = one_map(jnp.where(lane < DA_QK_DIM, q, zero))
    o2 = one_map(jnp.where(lane >= DA_QK_DIM, q, zero))
    o = o1 - lam_ref[0] * o2
    o = o * lax.rsqrt(jnp.mean(o * o, axis=-1, keepdims=True) + EPS) * g_ref[...]
    o_ref[...] = (o * out_scale).astype(o_ref.dtype)


def _attn(proj, lam, subln_g, batch, seq, lambda_init):
    T = proj.shape[0]
    tq = 256
    nq = seq // tq
    return pl.pallas_call(
        functools.partial(_attn_kernel, out_scale=1.0 - lambda_init),
        out_shape=jax.ShapeDtypeStruct((T, DA_HEADS * DA_V_DIM), BF16),
        grid=(batch, DA_HEADS, nq),
        in_specs=[pl.BlockSpec(memory_space=pltpu.SMEM),
                  pl.BlockSpec((tq, LANES), lambda b, h, i: (b * nq + i, COL_DA_Q + h)),
                  pl.BlockSpec((seq, LANES), lambda b, h, i: (b, COL_DA_K + h)),
                  pl.BlockSpec((seq, LANES), lambda b, h, i: (b, COL_DA_V + h)),
                  pl.BlockSpec((1, LANES), lambda b, h, i: (0, 0))],
        out_specs=pl.BlockSpec((tq, LANES), lambda b, h, i: (b * nq + i, h)),
        compiler_params=_params("arbitrary", "arbitrary", "arbitrary"),
        name="attn",
    )(lam, proj, proj, proj, subln_g)


def _lower_bound(tbl, layer):
    e = jnp.exp(tbl - jnp.max(tbl, axis=0, keepdims=True))
    sm = e / jnp.sum(e, axis=0, keepdims=True)
    return jnp.sum(sm[1:layer + 2], axis=0, keepdims=True)


HG_SCAN_ROWS = 256


def _hgrn_kernel(q_ref, ff_ref, fb_ref, i_ref, g_ref, tf_ref, tb_ref, ng_ref, o_ref,
                 ut_ref, st_ref, dec_ref, *, layer):
    L = HG_CHUNK
    S = q_ref.shape[0]
    C = S // L
    R = min(HG_SCAN_ROWS, S)
    row = lax.broadcasted_iota(jnp.int32, (R, R), 0)
    col = lax.broadcasted_iota(jnp.int32, (R, R), 1)
    same_chunk = (row // L) == (col // L)
    row_l = lax.broadcasted_iota(jnp.int32, (L, L), 0)
    col_l = lax.broadcasted_iota(jnp.int32, (L, L), 1)

    q3 = _silu(q_ref[...].astype(F32)).reshape(C, L, HG_DIM)
    v3 = i_ref[...].reshape(C, L, HG_DIM)
    v3t = jnp.swapaxes(v3.astype(F32), 1, 2).astype(BF16)

    def direction(raw_ref, tbl_ref, forward):
        lb = _lower_bound(tbl_ref[...], layer)
        f = lb + (1.0 - lb) * jax.nn.sigmoid(raw_ref[...].astype(F32))
        kk3 = (1.0 - f).reshape(C, L, HG_DIM)
        lf = jnp.log(f)
        hi = lf.astype(BF16)
        lo = (lf - hi.astype(F32)).astype(BF16)
        hl = jnp.concatenate([hi, lo], axis=1)
        tri = (same_chunk & ((row >= col) if forward else (row <= col))).astype(BF16)
        parts = []
        for j in range(S // R):
            bs = jnp.dot(tri, hl[j * R:(j + 1) * R], preferred_element_type=F32)
            parts.append(bs[:, :HG_DIM] + bs[:, HG_DIM:])
        b3 = jnp.concatenate(parts, axis=0).reshape(C, L, HG_DIM)
        ref_row, last_row = (L // 2 - 1, L - 1) if forward else (L // 2, 0)
        r = b3[:, ref_row:ref_row + 1, :]
        bl = b3[:, last_row:last_row + 1, :]
        qi = (q3 * jnp.exp(b3 - r)).astype(BF16)
        ki = (kk3 * jnp.exp(r - b3)).astype(BF16)
        a = jnp.einsum('ctd,csd->cts', qi, ki, preferred_element_type=F32)
        mask = (row_l >= col_l) if forward else (row_l <= col_l)
        a = jnp.where(mask[None], a, 0.0)
        o = jnp.einsum('cts,csv->ctv', a.astype(BF16), v3, preferred_element_type=F32)
        ku = (kk3 * jnp.exp(bl - b3)).astype(BF16)
        ut_ref[...] = jnp.einsum('cvs,csd->cvd', v3t, ku, preferred_element_type=F32)
        dec_ref[...] = jnp.exp(bl)

        def scan(i, state):
            c = i if forward else C - 1 - i
            st_ref[c] = state.astype(BF16)
            return state * dec_ref[c] + ut_ref[c]

        lax.fori_loop(0, C, scan, jnp.zeros((HG_DIM, HG_DIM), F32))
        qb = (q3 * jnp.exp(b3)).astype(BF16)
        return o + jnp.einsum('ctd,cvd->ctv', qb, st_ref[...], preferred_element_type=F32)

    o = direction(ff_ref, tf_ref, True) + direction(fb_ref, tb_ref, False)
    o = o.reshape(S, HG_DIM)
    o = o * lax.rsqrt(jnp.mean(o * o, axis=-1, keepdims=True) + EPS) * ng_ref[...]
    o_ref[...] = (o * _silu(g_ref[...].astype(F32))).astype(o_ref.dtype)


def _hgrn(proj, tbl_f, tbl_b, norm_g, batch, seq, layer):
    T = proj.shape[0]
    nrow = tbl_f.shape[0]
    C = seq // HG_CHUNK
    col = lambda off: pl.BlockSpec((seq, LANES), lambda b, h: (b, off + h))
    tbl = pl.BlockSpec((nrow, LANES), lambda b, h: (0, h))
    return pl.pallas_call(
        functools.partial(_hgrn_kernel, layer=layer),
        out_shape=jax.ShapeDtypeStruct((T, HG_HEADS * HG_DIM), BF16),
        grid=(batch, HG_HEADS),
        in_specs=[col(COL_HG_Q), col(COL_HG_FF), col(COL_HG_FB), col(COL_HG_I), col(COL_HG_G),
                  tbl, tbl, pl.BlockSpec((1, LANES), lambda b, h: (0, 0))],
        out_specs=pl.BlockSpec((seq, LANES), lambda b, h: (b, h)),
        scratch_shapes=[pltpu.VMEM((C, HG_DIM, HG_DIM), F32), pltpu.VMEM((C, HG_DIM, HG_DIM), BF16),
                        pltpu.VMEM((C, 1, HG_DIM), F32)],
        compiler_params=_params("arbitrary", "arbitrary"),
        name="hgrn",
    )(proj, proj, proj, proj, proj, tbl_f, tbl_b, norm_g)


def _mix_kernel(oa_ref, ob_ref, ga_ref, gb_ref, x_ref, gate_ref, shift_ref, scale_ref, g2_ref,
                wa_ref, wb_ref, wo_ref, wq_ref, x1_ref, h2_ref, qp_ref):
    ya = jnp.dot(oa_ref[...], wa_ref[...], preferred_element_type=F32)
    yb = jnp.dot(ob_ref[...], wb_ref[...], preferred_element_type=F32)
    y = (jax.nn.sigmoid(ga_ref[...].astype(F32)) * ya
         + jax.nn.sigmoid(gb_ref[...].astype(F32)) * yb)
    z = jnp.dot(y.astype(BF16), wo_ref[...], preferred_element_type=F32)
    x1 = x_ref[...] + gate_ref[0] * z
    x1_ref[...] = x1
    h2 = _modulated_norm(x1, g2_ref[...], shift_ref[0], scale_ref[0])
    h2_ref[...] = h2
    qp_ref[...] = jnp.dot(h2.astype(BF16), wq_ref[...], preferred_element_type=F32)


def _mix(o_a, o_b, proj, x2d, gate1, shift2, scale2, g2, wa, wb, wo, wq, seq):
    T, D = x2d.shape
    tm = 256
    gcol = proj.shape[1] // D - 2
    bidx = lambda i: ((i * tm) // seq, 0, 0)
    const = lambda shape: pl.BlockSpec(shape, lambda i: (0, 0), pipeline_mode=pl.Buffered(1))
    row = lambda w: pl.BlockSpec((tm, w), lambda i: (i, 0))
    return pl.pallas_call(
        _mix_kernel,
        out_shape=(jax.ShapeDtypeStruct((T, D), F32), jax.ShapeDtypeStruct((T, D), F32),
                   jax.ShapeDtypeStruct((T, wq.shape[1]), F32)),
        grid=(T // tm,),
        in_specs=[row(o_a.shape[1]), row(o_b.shape[1]),
                  pl.BlockSpec((tm, D), lambda i: (i, gcol)),
                  pl.BlockSpec((tm, D), lambda i: (i, gcol + 1)),
                  row(D),
                  pl.BlockSpec((1, 1, D), bidx), pl.BlockSpec((1, 1, D), bidx),
                  pl.BlockSpec((1, 1, D), bidx),
                  pl.BlockSpec((1, D), lambda i: (0, 0)),
                  const(wa.shape), const(wb.shape), const(wo.shape), const(wq.shape)],
        out_specs=(row(D), row(D), row(wq.shape[1])),
        compiler_params=_params("arbitrary"),
        name="mix",
    )(o_a, o_b, proj, proj, x2d, gate1, shift2, scale2, g2, wa, wb, wo, wq)


_BIG = 1e9


def _take_top(s, order, payload, n):
    vals, picked = [], []
    for _ in range(n):
        m = jnp.max(s, axis=0, keepdims=True)
        first = jnp.min(jnp.where(s == m, order, _BIG), axis=0, keepdims=True)
        hit = order == first
        vals.append(m)
        picked.append(first if payload is None else
                      jnp.sum(jnp.where(hit, payload, 0.0), axis=0, keepdims=True))
        s = jnp.where(hit, -jnp.inf, s)
    return vals, picked


def _staircase(v1, i1, v2, i2):
    k = PEER_TOPK
    W = v1[0].shape[1]
    row = lax.broadcasted_iota(jnp.int32, (8, W), 0)
    rowf = row.astype(F32)
    v1c, i1c = jnp.concatenate(v1, axis=0), jnp.concatenate(i1, axis=0)
    v2c, i2c = jnp.concatenate(v2, axis=0), jnp.concatenate(i2, axis=0)
    v2lo, i2lo = v2c[0:8], i2c[0:8]

    def sel3(x5, x6, x7):
        return jnp.where(row < 2, x5, jnp.where(row < 4, x6, x7))

    def shifted(x, n):
        return pltpu.roll(x, n, axis=0)

    pieces = [
        (v1[0], i1[0], v2lo, i2lo, rowf, None),
        (v1[0], i1[0], v2c[8:16], i2c[8:16], 8.0 + rowf, None),
        (v1[1], i1[1], v2lo, i2lo, k + rowf, None),
        (v1[2], i1[2], v2lo, i2lo, 2 * k + rowf, row < 5),
        (jnp.where(row < 4, v1[3], v1[4]), jnp.where(row < 4, i1[3], i1[4]),
         jnp.where(row < 4, v2lo, shifted(v2lo, 4)), jnp.where(row < 4, i2lo, shifted(i2lo, 4)),
         jnp.where(row < 4, 3 * k + rowf, 4 * k - 4 + rowf), row < 7),
        (sel3(v1[5], v1[6], v1[7]), sel3(i1[5], i1[6], i1[7]),
         sel3(v2lo, shifted(v2lo, 2), shifted(v2lo, 4)), sel3(i2lo, shifted(i2lo, 2), shifted(i2lo, 4)),
         sel3(5 * k + rowf, 6 * k - 2 + rowf, 7 * k - 4 + rowf), row < 6),
        (v1c[8:16], i1c[8:16], v2[0], i2[0], (8.0 + rowf) * k, None),
    ]
    cand, order, ids = [], [], []
    for va, ia, vb, ib, flat, valid in pieces:
        c = va + vb
        if valid is not None:
            c = jnp.where(valid, c, -jnp.inf)
            flat = jnp.where(valid, flat, _BIG)
        cand.append(c)
        order.append(flat)
        ids.append(ia * PEER_N_KEYS + ib)
    return (jnp.concatenate(cand, axis=0), jnp.concatenate(order, axis=0),
            jnp.concatenate(ids, axis=0))


def _topk_kernel(qp_ref, k1_ref, k2_ref, e_ref, g_ref):
    qp = qp_ref[...]
    s1 = lax.dot_general(k1_ref[0], qp[:, :PEER_HALF], _NT, preferred_element_type=F32)
    s2 = lax.dot_general(k2_ref[0], qp[:, PEER_HALF:], _NT, preferred_element_type=F32)
    key_rank = lax.broadcasted_iota(jnp.int32, s1.shape, 0).astype(F32)
    v1, i1 = _take_top(s1, key_rank, None, PEER_TOPK)
    v2, i2 = _take_top(s2, key_rank, None, PEER_TOPK)
    cand, order, ids = _staircase(v1, i1, v2, i2)
    sc, e = _take_top(cand, order, ids, PEER_TOPK)
    sc = jnp.concatenate(sc, axis=0)
    ex = jnp.exp(sc - sc[0:1, :])
    e_ref[...] = jnp.concatenate(e, axis=0).astype(jnp.int32)
    g = ex / jnp.sum(ex, axis=0, keepdims=True)
    for j in range(g_ref.shape[0]):
        g_ref[j] = g[:, j * LANES:(j + 1) * LANES]


def _topk(qp, keys1, keys2):
    T = qp.shape[0]
    tt = 256
    return pl.pallas_call(
        _topk_kernel,
        out_shape=(jax.ShapeDtypeStruct((PEER_KK, T), jnp.int32),
                   jax.ShapeDtypeStruct((T // LANES, PEER_KK, LANES), F32)),
        grid=(T // tt, PEER_HEADS),
        in_specs=[pl.BlockSpec((tt, 2 * PEER_HALF), lambda i, h: (i, h)),
                  pl.BlockSpec((1, PEER_N_KEYS, PEER_HALF), lambda i, h: (h, 0, 0)),
                  pl.BlockSpec((1, PEER_N_KEYS, PEER_HALF), lambda i, h: (h, 0, 0))],
        out_specs=(pl.BlockSpec((PEER_TOPK, tt), lambda i, h: (h, i)),
                   pl.BlockSpec((tt // LANES, PEER_TOPK, LANES), lambda i, h: (i, h, 0))),
        compiler_params=_params("arbitrary", "arbitrary"),
        name="topk",
    )(qp, keys1, keys2)


PEER_TB = 256
PEER_SLOTS = 8


def _pack_expert_tables(u, v):
    ub = lax.bitcast_convert_type(u.astype(BF16), jnp.uint16).astype(jnp.uint32)
    vb = lax.bitcast_convert_type(v.astype(BF16), jnp.uint16).astype(jnp.uint32)
    return ((vb << 16) | ub).reshape(u.shape[0], 1, u.shape[1])


def _peer_kernel(e_ref, h2_ref, gt_ref, x1_ref, gate_ref, fg_ref, w_hbm, o_ref,
                 wbuf, sem, acc_ref):
    D = h2_ref.shape[1]
    col_blocks = [slice(cb * LANES, (cb + 1) * LANES) for cb in range(D // LANES)]

    def issue(t, slot):
        for k in range(PEER_KK):
            pltpu.make_async_copy(w_hbm.at[e_ref[t, k]], wbuf.at[slot, pl.ds(k, 1)],
                                  sem.at[slot]).start(priority=k % 2)

    def wait(slot):
        pltpu.make_async_copy(w_hbm.at[pl.ds(0, PEER_KK), 0], wbuf.at[slot], sem.at[slot]).wait()

    lane = lax.broadcasted_iota(jnp.int32, (PEER_KK, LANES), 1)

    def expert_mlp(t, slot):
        xt = h2_ref[pl.ds(t, 1), :]
        part = jnp.zeros((PEER_KK, LANES), F32)
        for cols in col_blocks:
            part += lax.bitcast_convert_type(wbuf[slot, :, cols] << 16, F32) * xt[:, cols]
        dots = jnp.sum(part, axis=1, keepdims=True)
        gcol = jnp.sum(jnp.where(lane == t % LANES, gt_ref[t // LANES], 0.0), axis=1, keepdims=True)
        a = 0.5 * dots * (1.0 + lax.erf(dots * (2.0 ** -0.5))) * gcol
        a_b = jnp.broadcast_to(a, (PEER_KK, LANES))
        pieces = []
        for cols in col_blocks:
            vrows = lax.bitcast_convert_type(wbuf[slot, :, cols] & jnp.uint32(0xFFFF0000), F32)
            pieces.append(jnp.sum(a_b * vrows, axis=0, keepdims=True))
        return jnp.concatenate(pieces, axis=1)

    for slot in range(PEER_SLOTS):
        issue(slot, slot)

    def steady(g, carry):
        for slot in range(PEER_SLOTS):
            t = g * PEER_SLOTS + slot
            wait(slot)
            row = expert_mlp(t, slot)
            issue(t + PEER_SLOTS, slot)
            acc_ref[pl.ds(t, 1), :] = row
        return carry

    n_groups = PEER_TB // PEER_SLOTS
    lax.fori_loop(0, n_groups - 1, steady, 0)
    for slot in range(PEER_SLOTS):
        t = (n_groups - 1) * PEER_SLOTS + slot
        wait(slot)
        acc_ref[pl.ds(t, 1), :] = expert_mlp(t, slot)
    x2 = x1_ref[...] + gate_ref[0] * acc_ref[...]
    o_ref[...] = x2 * lax.rsqrt(jnp.mean(x2 * x2, axis=-1, keepdims=True) + EPS) * fg_ref[...]


def _peer(e_tok, h2, g_t, x1, gate2, final_g, w_packed, seq):
    T, D = x1.shape
    tb = PEER_TB
    bidx = lambda i: ((i * tb) // seq, 0, 0)
    row = pl.BlockSpec((tb, D), lambda i: (i, 0))
    return pl.pallas_call(
        _peer_kernel,
        out_shape=jax.ShapeDtypeStruct((T, D), F32),
        grid=(T // tb,),
        in_specs=[pl.BlockSpec((tb, PEER_KK), lambda i: (i, 0), memory_space=pltpu.SMEM),
                  row,
                  pl.BlockSpec((tb // LANES, PEER_KK, LANES), lambda i: (i, 0, 0)),
                  row,
                  pl.BlockSpec((1, 1, D), bidx),
                  pl.BlockSpec((1, D), lambda i: (0, 0)),
                  pl.BlockSpec(memory_space=pl.ANY)],
        out_specs=row,
        scratch_shapes=[pltpu.VMEM((PEER_SLOTS, PEER_KK, D), jnp.uint32),
                        pltpu.SemaphoreType.DMA((PEER_SLOTS,)),
                        pltpu.VMEM((tb, D), F32)],
        compiler_params=_params("arbitrary"),
        name="peer",
    )(e_tok, h2, g_t, x1, gate2, final_g, w_packed)


def kernel(x, c, positions, w_ada, b_ada, norm1_g, w_in, diff_lq1, diff_lk1, diff_lq2, diff_lk2,
           diff_subln_g, hgrn_lb_fwd, hgrn_lb_bwd, hgrn_norm_g, w_branch_attn, w_branch_hgrn,
           w_out, norm2_g, peer_wq, peer_keys1, peer_keys2, peer_u, peer_v, final_norm_g):
    B, S, D = x.shape
    T = B * S
    depth = w_ada.shape[0]
    assert depth == 1, "the fused final-norm epilogue assumes a single layer"
    rc, rs1, rs2 = _rope_lane_tables(positions)
    x2d = x.reshape(T, D)
    l = 0
    ada = _ada(c, w_ada[l], b_ada[l])
    shift1, scale1, gate1, shift2, scale2, gate2 = (
        ada[:, i * D:(i + 1) * D].reshape(B, 1, D) for i in range(6))
    proj = _proj(x2d, shift1, scale1, norm1_g[l].reshape(1, D), w_in[l].astype(BF16),
                 rc, rs1, rs2, S)
    lambda_init = 0.8 - 0.6 * float(np.exp(-0.3 * l))
    lam = (jnp.exp(jnp.sum(diff_lq1[l] * diff_lk1[l])) - jnp.exp(jnp.sum(diff_lq2[l] * diff_lk2[l]))
           + lambda_init).reshape(1).astype(F32)
    o_a = _attn(proj, lam, diff_subln_g[l].reshape(1, DA_V_DIM), B, S, lambda_init)
    o_b = _hgrn(proj, hgrn_lb_fwd, hgrn_lb_bwd, hgrn_norm_g[l].reshape(1, HG_DIM), B, S, l)
    x1, h2, qp = _mix(o_a, o_b, proj, x2d, gate1, shift2, scale2, norm2_g[l].reshape(1, D),
                      w_branch_attn[l].astype(BF16), w_branch_hgrn[l].astype(BF16),
                      w_out[l].astype(BF16), peer_wq[l].astype(BF16), S)
    e_t, g_t = _topk(qp, peer_keys1[l], peer_keys2[l])
    out = _peer(e_t.T, h2, g_t, x1, gate2, final_norm_g.reshape(1, D),
                _pack_expert_tables(peer_u[l], peer_v[l]), S)
    return out.reshape(B, S, D)
```

```python
import functools

import numpy as np
import jax
import jax.numpy as jnp
from jax import lax
from jax.experimental import pallas as pl
from jax.experimental.pallas import tpu as pltpu

F32 = jnp.float32
BF16 = jnp.bfloat16

EPS = 1e-6
DA_HEADS = 8
DA_QK_DIM = 64
DA_V_DIM = 128
ROPE_DIM = DA_QK_DIM // 4
ROPE_HALF = ROPE_DIM // 2
ROPE_THETA = 500000.0
HG_HEADS = 8
HG_DIM = 128
HG_CHUNK = 64
PEER_HEADS = 8
PEER_N_KEYS = 128
PEER_HALF = 128
PEER_TOPK = 16
PEER_KK = PEER_HEADS * PEER_TOPK

LANES = 128
VMEM_LIMIT = 56 * 1024 * 1024

COL_DA_Q, COL_DA_K, COL_DA_V = 0, 8, 16
COL_HG_Q, COL_HG_FF, COL_HG_FB, COL_HG_I, COL_HG_G = 24, 32, 40, 48, 56

_NT = (((1,), (1,)), ((), ()))
_TN = (((0,), (0,)), ((), ()))


def _params(*sem):
    return pltpu.CompilerParams(dimension_semantics=sem, vmem_limit_bytes=VMEM_LIMIT)


def _silu(x):
    return x * jax.nn.sigmoid(x)


def _ada_kernel(c_ref, w_ref, b_ref, o_ref):
    s = _silu(c_ref[...])
    s_hi = s.astype(BF16)
    s_lo = (s - s_hi.astype(F32)).astype(BF16)
    w = w_ref[...]
    w_hi = w.astype(BF16)
    w_lo = (w - w_hi.astype(F32)).astype(BF16)
    acc = jnp.dot(s_hi, w_hi, preferred_element_type=F32)
    acc += jnp.dot(s_lo, w_hi, preferred_element_type=F32)
    acc += jnp.dot(s_hi, w_lo, preferred_element_type=F32)
    o_ref[...] = acc + b_ref[...]


def _ada(c, w, b):
    B, D = c.shape
    N = w.shape[1]
    tn = 1024
    return pl.pallas_call(
        _ada_kernel,
        out_shape=jax.ShapeDtypeStruct((B, N), F32),
        grid=(N // tn,),
        in_specs=[pl.BlockSpec((B, D), lambda j: (0, 0)),
                  pl.BlockSpec((D, tn), lambda j: (0, j)),
                  pl.BlockSpec((1, tn), lambda j: (0, j))],
        out_specs=pl.BlockSpec((B, tn), lambda j: (0, j)),
        compiler_params=_params("arbitrary"),
        name="ada",
    )(c, w, b.reshape(1, N))


def _modulated_norm(xf, g, shift, scale):
    y = xf * lax.rsqrt(jnp.mean(xf * xf, axis=-1, keepdims=True) + EPS) * g
    return y * (1.0 + scale) + shift


def _proj_kernel(x_ref, shift_ref, scale_ref, g_ref, w_ref, rc_ref, rs1_ref, rs2_ref, o_ref, h_ref):
    j = pl.program_id(1)

    @pl.when(j == 0)
    def _():
        h_ref[...] = _modulated_norm(x_ref[...], g_ref[...], shift_ref[0], scale_ref[0]).astype(BF16)

    acc = jnp.dot(h_ref[...], w_ref[...], preferred_element_type=F32)
    tn = acc.shape[1]

    def store_rotated(mult):
        rc, rs1, rs2 = rc_ref[...], rs1_ref[...], rs2_ref[...]
        for cb in range(tn // LANES):
            t = acc[:, cb * LANES:(cb + 1) * LANES]
            r = (t * rc + pltpu.roll(t, ROPE_HALF, axis=1) * rs1
                 + pltpu.roll(t, LANES - ROPE_HALF, axis=1) * rs2)
            o_ref[:, cb * LANES:(cb + 1) * LANES] = (r * mult).astype(o_ref.dtype)

    @pl.when(j == 0)
    def _():
        store_rotated(DA_QK_DIM ** -0.5)

    @pl.when(j == 1)
    def _():
        store_rotated(1.0)

    @pl.when(j >= 2)
    def _():
        o_ref[...] = acc.astype(o_ref.dtype)


def _proj(x2d, shift, scale, g, w_bf16, rc, rs1, rs2, seq):
    T, D = x2d.shape
    N = w_bf16.shape[1]
    tm, tn = min(1024, seq), 1024
    assert seq % tm == 0 and N % tn == 0 and tn == DA_HEADS * 2 * DA_QK_DIM
    bidx = lambda i, j: ((i * tm) // seq, 0, 0)
    return pl.pallas_call(
        _proj_kernel,
        out_shape=jax.ShapeDtypeStruct((T, N), BF16),
        grid=(T // tm, N // tn),
        in_specs=[pl.BlockSpec((tm, D), lambda i, j: (i, 0)),
                  pl.BlockSpec((1, 1, D), bidx),
                  pl.BlockSpec((1, 1, D), bidx),
                  pl.BlockSpec((1, D), lambda i, j: (0, 0)),
                  pl.BlockSpec((D, tn), lambda i, j: (0, j)),
                  pl.BlockSpec((tm, LANES), lambda i, j: (i, 0)),
                  pl.BlockSpec((tm, LANES), lambda i, j: (i, 0)),
                  pl.BlockSpec((tm, LANES), lambda i, j: (i, 0))],
        out_specs=pl.BlockSpec((tm, tn), lambda i, j: (i, j)),
        scratch_shapes=[pltpu.VMEM((tm, D), BF16)],
        compiler_params=_params("arbitrary", "arbitrary"),
        name="proj",
    )(x2d, shift, scale, g, w_bf16, rc, rs1, rs2)


def _rope_lane_tables(positions):
    B, S = positions.shape
    T = B * S
    inv = ROPE_THETA ** (-jnp.arange(0, ROPE_DIM, 2, dtype=F32) / ROPE_DIM)
    ang = positions.astype(F32).reshape(T, 1) * inv
    cos, sin = jnp.cos(ang), jnp.sin(ang)
    one = jnp.ones((T, DA_QK_DIM - ROPE_DIM), F32)
    zero8 = jnp.zeros((T, ROPE_HALF), F32)
    zero48 = jnp.zeros((T, DA_QK_DIM - ROPE_DIM), F32)
    rc = jnp.concatenate([cos, cos, one], axis=1)
    rs1 = jnp.concatenate([zero8, sin, zero48], axis=1)
    rs2 = jnp.concatenate([-sin, zero8, zero48], axis=1)
    rep = LANES // DA_QK_DIM
    return jnp.tile(rc, (1, rep)), jnp.tile(rs1, (1, rep)), jnp.tile(rs2, (1, rep))


def _attn_kernel(lam_ref, q_ref, k_ref, v_ref, g_ref, o_ref, *, out_scale):
    q = q_ref[...]
    k = k_ref[...]
    v = v_ref[...]
    lane = lax.broadcasted_iota(jnp.int32, q.shape, 1)
    zero = jnp.zeros_like(q)

    def one_map(qm):
        s = lax.dot_general(qm, k, _NT, preferred_element_type=F32)
        e = jnp.exp(s - jnp.max(s, axis=-1, keepdims=True))
        l = jnp.sum(e, axis=-1, keepdims=True)
        return jnp.dot(e.astype(v.dtype), v, preferred_element_type=F32) / l

    o1 = one_map(jnp.where(lane < DA_QK_DIM, q, zero))
    o2 = one_map(jnp.where(lane >= DA_QK_DIM, q, zero))
    o = o1 - lam_ref[0] * o2
    o = o * lax.rsqrt(jnp.mean(o * o, axis=-1, keepdims=True) + EPS) * g_ref[...]
    o_ref[...] = (o * out_scale).astype(o_ref.dtype)


def _attn(proj, lam, subln_g, batch, seq, lambda_init):
    T = proj.shape[0]
    tq = 256
    nq = seq // tq
    return pl.pallas_call(
        functools.partial(_attn_kernel, out_scale=1.0 - lambda_init),
        out_shape=jax.ShapeDtypeStruct((T, DA_HEADS * DA_V_DIM), BF16),
        grid=(batch, DA_HEADS, nq),
        in_specs=[pl.BlockSpec(memory_space=pltpu.SMEM),
                  pl.BlockSpec((tq, LANES), lambda b, h, i: (b * nq + i, COL_DA_Q + h)),
                  pl.BlockSpec((seq, LANES), lambda b, h, i: (b, COL_DA_K + h)),
                  pl.BlockSpec((seq, LANES), lambda b, h, i: (b, COL_DA_V + h)),
                  pl.BlockSpec((1, LANES), lambda b, h, i: (0, 0))],
        out_specs=pl.BlockSpec((tq, LANES), lambda b, h, i: (b * nq + i, h)),
        compiler_params=_params("arbitrary", "arbitrary", "arbitrary"),
        name="attn",
    )(lam, proj, proj, proj, subln_g)


def _lower_bound(tbl, layer):
    e = jnp.exp(tbl - jnp.max(tbl, axis=0, keepdims=True))
    sm = e / jnp.sum(e, axis=0, keepdims=True)
    return jnp.sum(sm[1:layer + 2], axis=0, keepdims=True)


HG_SCAN_ROWS = 256


def _hgrn_kernel(q_ref, ff_ref, fb_ref, i_ref, g_ref, tf_ref, tb_ref, ng_ref, o_ref,
                 ut_ref, st_ref, dec_ref, *, layer):
    L = HG_CHUNK
    S = q_ref.shape[0]
    C = S // L
    R = min(HG_SCAN_ROWS, S)
    row = lax.broadcasted_iota(jnp.int32, (R, R), 0)
    col = lax.broadcasted_iota(jnp.int32, (R, R), 1)
    same_chunk = (row // L) == (col // L)
    row_l = lax.broadcasted_iota(jnp.int32, (L, L), 0)
    col_l = lax.broadcasted_iota(jnp.int32, (L, L), 1)

    q3 = _silu(q_ref[...].astype(F32)).reshape(C, L, HG_DIM)
    v3 = i_ref[...].reshape(C, L, HG_DIM)
    v3t = jnp.swapaxes(v3.astype(F32), 1, 2).astype(BF16)

    def direction(raw_ref, tbl_ref, forward):
        lb = _lower_bound(tbl_ref[...], layer)
        f = lb + (1.0 - lb) * jax.nn.sigmoid(raw_ref[...].astype(F32))
        kk3 = (1.0 - f).reshape(C, L, HG_DIM)
        lf = jnp.log(f)
        hi = lf.astype(BF16)
        lo = (lf - hi.astype(F32)).astype(BF16)
        hl = jnp.concatenate([hi, lo], axis=1)
        tri = (same_chunk & ((row >= col) if forward else (row <= col))).astype(BF16)
        parts = []
        for j in range(S // R):
            bs = jnp.dot(tri, hl[j * R:(j + 1) * R], preferred_element_type=F32)
            parts.append(bs[:, :HG_DIM] + bs[:, HG_DIM:])
        b3 = jnp.concatenate(parts, axis=0).reshape(C, L, HG_DIM)
        ref_row, last_row = (L // 2 - 1, L - 1) if forward else (L // 2, 0)
        r = b3[:, ref_row:ref_row + 1, :]
        bl = b3[:, last_row:last_row + 1, :]
        qi = (q3 * jnp.exp(b3 - r)).astype(BF16)
        ki = (kk3 * jnp.exp(r - b3)).astype(BF16)
        a = jnp.einsum('ctd,csd->cts', qi, ki, preferred_element_type=F32)
        mask = (row_l >= col_l) if forward else (row_l <= col_l)
        a = jnp.where(mask[None], a, 0.0)
        o = jnp.einsum('cts,csv->ctv', a.astype(BF16), v3, preferred_element_type=F32)
        ku = (kk3 * jnp.exp(bl - b3)).astype(BF16)
        ut_ref[...] = jnp.einsum('cvs,csd->cvd', v3t, ku, preferred_element_type=F32)
        dec_ref[...] = jnp.exp(bl)

        def scan(i, state):
            c = i if forward else C - 1 - i
            st_ref[c] = state.astype(BF16)
            return state * dec_ref[c] + ut_ref[c]

        lax.fori_loop(0, C, scan, jnp.zeros((HG_DIM, HG_DIM), F32))
        qb = (q3 * jnp.exp(b3)).astype(BF16)
        return o + jnp.einsum('ctd,cvd->ctv', qb, st_ref[...], preferred_element_type=F32)

    o = direction(ff_ref, tf_ref, True) + direction(fb_ref, tb_ref, False)
    o = o.reshape(S, HG_DIM)
    o = o * lax.rsqrt(jnp.mean(o * o, axis=-1, keepdims=True) + EPS) * ng_ref[...]
    o_ref[...] = (o * _silu(g_ref[...].astype(F32))).astype(o_ref.dtype)


def _hgrn(proj, tbl_f, tbl_b, norm_g, batch, seq, layer):
    T = proj.shape[0]
    nrow = tbl_f.shape[0]
    C = seq // HG_CHUNK
    col = lambda off: pl.BlockSpec((seq, LANES), lambda b, h: (b, off + h))
    tbl = pl.BlockSpec((nrow, LANES), lambda b, h: (0, h))
    return pl.pallas_call(
        functools.partial(_hgrn_kernel, layer=layer),
        out_shape=jax.ShapeDtypeStruct((T, HG_HEADS * HG_DIM), BF16),
        grid=(batch, HG_HEADS),
        in_specs=[col(COL_HG_Q), col(COL_HG_FF), col(COL_HG_FB), col(COL_HG_I), col(COL_HG_G),
                  tbl, tbl, pl.BlockSpec((1, LANES), lambda b, h: (0, 0))],
        out_specs=pl.BlockSpec((seq, LANES), lambda b, h: (b, h)),
        scratch_shapes=[pltpu.VMEM((C, HG_DIM, HG_DIM), F32), pltpu.VMEM((C, HG_DIM, HG_DIM), BF16),
                        pltpu.VMEM((C, 1, HG_DIM), F32)],
        compiler_params=_params("arbitrary", "arbitrary"),
        name="hgrn",
    )(proj, proj, proj, proj, proj, tbl_f, tbl_b, norm_g)


def _mix_kernel(oa_ref, ob_ref, ga_ref, gb_ref, x_ref, gate_ref, shift_ref, scale_ref, g2_ref,
                wa_ref, wb_ref, wo_ref, wq_ref, x1_ref, h2_ref, qp_ref):
    ya = jnp.dot(oa_ref[...], wa_ref[...], preferred_element_type=F32)
    yb = jnp.dot(ob_ref[...], wb_ref[...], preferred_element_type=F32)
    y = (jax.nn.sigmoid(ga_ref[...].astype(F32)) * ya
         + jax.nn.sigmoid(gb_ref[...].astype(F32)) * yb)
    z = jnp.dot(y.astype(BF16), wo_ref[...], preferred_element_type=F32)
    x1 = x_ref[...] + gate_ref[0] * z
    x1_ref[...] = x1
    h2 = _modulated_norm(x1, g2_ref[...], shift_ref[0], scale_ref[0])
    h2_ref[...] = h2
    qp_ref[...] = jnp.dot(h2.astype(BF16), wq_ref[...], preferred_element_type=F32)


def _mix(o_a, o_b, proj, x2d, gate1, shift2, scale2, g2, wa, wb, wo, wq, seq):
    T, D = x2d.shape
    tm = 256
    gcol = proj.shape[1] // D - 2
    bidx = lambda i: ((i * tm) // seq, 0, 0)
    const = lambda shape: pl.BlockSpec(shape, lambda i: (0, 0), pipeline_mode=pl.Buffered(1))
    row = lambda w: pl.BlockSpec((tm, w), lambda i: (i, 0))
    return pl.pallas_call(
        _mix_kernel,
        out_shape=(jax.ShapeDtypeStruct((T, D), F32), jax.ShapeDtypeStruct((T, D), F32),
                   jax.ShapeDtypeStruct((T, wq.shape[1]), F32)),
        grid=(T // tm,),
        in_specs=[row(o_a.shape[1]), row(o_b.shape[1]),
                  pl.BlockSpec((tm, D), lambda i: (i, gcol)),
                  pl.BlockSpec((tm, D), lambda i: (i, gcol + 1)),
                  row(D),
                  pl.BlockSpec((1, 1, D), bidx), pl.BlockSpec((1, 1, D), bidx),
                  pl.BlockSpec((1, 1, D), bidx),
                  pl.BlockSpec((1, D), lambda i: (0, 0)),
                  const(wa.shape), const(wb.shape), const(wo.shape), const(wq.shape)],
        out_specs=(row(D), row(D), row(wq.shape[1])),
        compiler_params=_params("arbitrary"),
        name="mix",
    )(o_a, o_b, proj, proj, x2d, gate1, shift2, scale2, g2, wa, wb, wo, wq)


_BIG = 1e9


def _take_top(s, order, payload, n):
    vals, picked = [], []
    for _ in range(n):
        m = jnp.max(s, axis=0, keepdims=True)
        first = jnp.min(jnp.where(s == m, order, _BIG), axis=0, keepdims=True)
        hit = order == first
        vals.append(m)
        picked.append(first if payload is None else
                      jnp.sum(jnp.where(hit, payload, 0.0), axis=0, keepdims=True))
        s = jnp.where(hit, -jnp.inf, s)
    return vals, picked


def _staircase(v1, i1, v2, i2):
    k = PEER_TOPK
    W = v1[0].shape[1]
    row = lax.broadcasted_iota(jnp.int32, (8, W), 0)
    rowf = row.astype(F32)
    v1c, i1c = jnp.concatenate(v1, axis=0), jnp.concatenate(i1, axis=0)
    v2c, i2c = jnp.concatenate(v2, axis=0), jnp.concatenate(i2, axis=0)
    v2lo, i2lo = v2c[0:8], i2c[0:8]

    def sel3(x5, x6, x7):
        return jnp.where(row < 2, x5, jnp.where(row < 4, x6, x7))

    def shifted(x, n):
        return pltpu.roll(x, n, axis=0)

    pieces = [
        (v1[0], i1[0], v2lo, i2lo, rowf, None),
        (v1[0], i1[0], v2c[8:16], i2c[8:16], 8.0 + rowf, None),
        (v1[1], i1[1], v2lo, i2lo, k + rowf, None),
        (v1[2], i1[2], v2lo, i2lo, 2 * k + rowf, row < 5),
        (jnp.where(row < 4, v1[3], v1[4]), jnp.where(row < 4, i1[3], i1[4]),
         jnp.where(row < 4, v2lo, shifted(v2lo, 4)), jnp.where(row < 4, i2lo, shifted(i2lo, 4)),
         jnp.where(row < 4, 3 * k + rowf, 4 * k - 4 + rowf), row < 7),
        (sel3(v1[5], v1[6], v1[7]), sel3(i1[5], i1[6], i1[7]),
         sel3(v2lo, shifted(v2lo, 2), shifted(v2lo, 4)), sel3(i2lo, shifted(i2lo, 2), shifted(i2lo, 4)),
         sel3(5 * k + rowf, 6 * k - 2 + rowf, 7 * k - 4 + rowf), row < 6),
        (v1c[8:16], i1c[8:16], v2[0], i2[0], (8.0 + rowf) * k, None),
    ]
    cand, order, ids = [], [], []
    for va, ia, vb, ib, flat, valid in pieces:
        c = va + vb
        if valid is not None:
            c = jnp.where(valid, c, -jnp.inf)
            flat = jnp.where(valid, flat, _BIG)
        cand.append(c)
        order.append(flat)
        ids.append(ia * PEER_N_KEYS + ib)
    return (jnp.concatenate(cand, axis=0), jnp.concatenate(order, axis=0),
            jnp.concatenate(ids, axis=0))


def _topk_kernel(qp_ref, k1_ref, k2_ref, e_ref, g_ref):
    qp = qp_ref[...]
    s1 = lax.dot_general(k1_ref[0], qp[:, :PEER_HALF], _NT, preferred_element_type=F32)
    s2 = lax.dot_general(k2_ref[0], qp[:, PEER_HALF:], _NT, preferred_element_type=F32)
    key_rank = lax.broadcasted_iota(jnp.int32, s1.shape, 0).astype(F32)
    v1, i1 = _take_top(s1, key_rank, None, PEER_TOPK)
    v2, i2 = _take_top(s2, key_rank, None, PEER_TOPK)
    cand, order, ids = _staircase(v1, i1, v2, i2)
    sc, e = _take_top(cand, order, ids, PEER_TOPK)
    sc = jnp.concatenate(sc, axis=0)
    ex = jnp.exp(sc - sc[0:1, :])
    e_ref[...] = jnp.concatenate(e, axis=0).astype(jnp.int32)
    g = ex / jnp.sum(ex, axis=0, keepdims=True)
    for j in range(g_ref.shape[0]):
        g_ref[j] = g[:, j * LANES:(j + 1) * LANES]


def _topk(qp, keys1, keys2):
    T = qp.shape[0]
    tt = 256
    return pl.pallas_call(
        _topk_kernel,
        out_shape=(jax.ShapeDtypeStruct((PEER_KK, T), jnp.int32),
                   jax.ShapeDtypeStruct((T // LANES, PEER_KK, LANES), F32)),
        grid=(T // tt, PEER_HEADS),
        in_specs=[pl.BlockSpec((tt, 2 * PEER_HALF), lambda i, h: (i, h)),
                  pl.BlockSpec((1, PEER_N_KEYS, PEER_HALF), lambda i, h: (h, 0, 0)),
                  pl.BlockSpec((1, PEER_N_KEYS, PEER_HALF), lambda i, h: (h, 0, 0))],
        out_specs=(pl.BlockSpec((PEER_TOPK, tt), lambda i, h: (h, i)),
                   pl.BlockSpec((tt // LANES, PEER_TOPK, LANES), lambda i, h: (i, h, 0))),
        compiler_params=_params("arbitrary", "arbitrary"),
        name="topk",
    )(qp, keys1, keys2)


PEER_TB = 256
PEER_SLOTS = 8
PEER_BF16_TERMS = 4


def _bf16_bits(x):
    b = lax.bitcast_convert_type(x, jnp.uint32)
    return (b + jnp.uint32(0x7FFF) + ((b >> 16) & jnp.uint32(1))) >> 16


def _pack_words(x):
    half = x.shape[1] // 2
    return (_bf16_bits(x[:, half:]) << 16) | _bf16_bits(x[:, :half])


def _pack_kernel(u_ref, v_ref, o_ref):
    half = u_ref.shape[1] // 2
    o_ref[:, 0, :half] = _pack_words(u_ref[...])
    o_ref[:, 0, half:] = _pack_words(v_ref[...])


def _pack_expert_tables(u, v):
    N, D = u.shape
    tr = 256
    return pl.pallas_call(
        _pack_kernel,
        out_shape=jax.ShapeDtypeStruct((N, 1, D), jnp.uint32),
        grid=(N // tr,),
        in_specs=[pl.BlockSpec((tr, D), lambda i: (i, 0)), pl.BlockSpec((tr, D), lambda i: (i, 0))],
        out_specs=pl.BlockSpec((tr, 1, D), lambda i: (i, 0, 0)),
        compiler_params=_params("arbitrary"),
        name="pack",
    )(u, v)


def _peer_kernel(e_ref, h2_ref, gt_ref, x1_ref, gate_ref, fg_ref, w_hbm, o_ref,
                 wbuf, sem, acc_ref, xw_ref):
    D = h2_ref.shape[1]
    half = D // 2
    half_blocks = [slice(cb * LANES, (cb + 1) * LANES) for cb in range(half // LANES)]
    high_bits = jnp.uint32(0xFFFF0000)

    def unpack(words):
        return (lax.bitcast_convert_type(words << 16, F32),
                lax.bitcast_convert_type(words & high_bits, F32))

    def issue(t, slot):
        for k in range(PEER_KK):
            pltpu.make_async_copy(w_hbm.at[e_ref[t, k]], wbuf.at[slot, pl.ds(k, 1)],
                                  sem.at[slot]).start(priority=k % 2)

    def wait(slot):
        pltpu.make_async_copy(w_hbm.at[pl.ds(0, PEER_KK), 0], wbuf.at[slot], sem.at[slot]).wait()

    lane = lax.broadcasted_iota(jnp.int32, (PEER_KK, LANES), 1)

    def expert_mlp(t, slot):
        xw = xw_ref[pl.ds(t, 1), :]
        part = jnp.zeros((PEER_KK, LANES), F32)
        for g0 in range(0, len(half_blocks), PEER_BF16_TERMS):
            prods = []
            for cols in half_blocks[g0:g0 + PEER_BF16_TERMS]:
                xb = pltpu.bitcast(jnp.broadcast_to(xw[:, cols], (PEER_KK, LANES)), BF16)
                prods.append(pltpu.bitcast(wbuf[slot, :, cols], BF16) * xb)
            while len(prods) > 1:
                prods = [a + b for a, b in zip(prods[::2], prods[1::2])]
            s_lo, s_hi = unpack(pltpu.bitcast(prods[0], jnp.uint32))
            part += s_lo + s_hi
        dots = jnp.sum(part, axis=1, keepdims=True)
        gcol = jnp.sum(jnp.where(lane == t % LANES, gt_ref[t // LANES], 0.0), axis=1, keepdims=True)
        a = 0.5 * dots * (1.0 + lax.erf(dots * (2.0 ** -0.5))) * gcol
        a_b = jnp.broadcast_to(a, (PEER_KK, LANES))
        lo_pieces, hi_pieces = [], []
        for cols in half_blocks:
            v_lo, v_hi = unpack(wbuf[slot, :, half + cols.start:half + cols.stop])
            lo_pieces.append(jnp.sum(a_b * v_lo, axis=0, keepdims=True))
            hi_pieces.append(jnp.sum(a_b * v_hi, axis=0, keepdims=True))
        return jnp.concatenate(lo_pieces + hi_pieces, axis=1)

    for slot in range(PEER_SLOTS):
        issue(slot, slot)

    xw_ref[...] = _pack_words(h2_ref[...])

    def steady(g, carry):
        for slot in range(PEER_SLOTS):
            t = g * PEER_SLOTS + slot
            wait(slot)
            row = expert_mlp(t, slot)
            issue(t + PEER_SLOTS, slot)
            acc_ref[pl.ds(t, 1), :] = row
        return carry

    n_groups = PEER_TB // PEER_SLOTS
    lax.fori_loop(0, n_groups - 1, steady, 0)
    for slot in range(PEER_SLOTS):
        t = (n_groups - 1) * PEER_SLOTS + slot
        wait(slot)
        acc_ref[pl.ds(t, 1), :] = expert_mlp(t, slot)
    x2 = x1_ref[...] + gate_ref[0] * acc_ref[...]
    o_ref[...] = x2 * lax.rsqrt(jnp.mean(x2 * x2, axis=-1, keepdims=True) + EPS) * fg_ref[...]


def _peer(e_tok, h2, g_t, x1, gate2, final_g, w_packed, seq):
    T, D = x1.shape
    tb = PEER_TB
    bidx = lambda i: ((i * tb) // seq, 0, 0)
    row = pl.BlockSpec((tb, D), lambda i: (i, 0))
    return pl.pallas_call(
        _peer_kernel,
        out_shape=jax.ShapeDtypeStruct((T, D), F32),
        grid=(T // tb,),
        in_specs=[pl.BlockSpec((tb, PEER_KK), lambda i: (i, 0), memory_space=pltpu.SMEM),
                  row,
                  pl.BlockSpec((tb // LANES, PEER_KK, LANES), lambda i: (i, 0, 0)),
                  row,
                  pl.BlockSpec((1, 1, D), bidx),
                  pl.BlockSpec((1, D), lambda i: (0, 0)),
                  pl.BlockSpec(memory_space=pl.ANY)],
        out_specs=row,
        scratch_shapes=[pltpu.VMEM((PEER_SLOTS, PEER_KK, D), jnp.uint32),
                        pltpu.SemaphoreType.DMA((PEER_SLOTS,)),
                        pltpu.VMEM((tb, D), F32),
                        pltpu.VMEM((tb, D // 2), jnp.uint32)],
        compiler_params=_params("arbitrary"),
        name="peer",
    )(e_tok, h2, g_t, x1, gate2, final_g, w_packed)


def kernel(x, c, positions, w_ada, b_ada, norm1_g, w_in, diff_lq1, diff_lk1, diff_lq2, diff_lk2,
           diff_subln_g, hgrn_lb_fwd, hgrn_lb_bwd, hgrn_norm_g, w_branch_attn, w_branch_hgrn,
           w_out, norm2_g, peer_wq, peer_keys1, peer_keys2, peer_u, peer_v, final_norm_g):
    B, S, D = x.shape
    T = B * S
    depth = w_ada.shape[0]
    assert depth == 1, "the fused final-norm epilogue assumes a single layer"
    rc, rs1, rs2 = _rope_lane_tables(positions)
    x2d = x.reshape(T, D)
    l = 0
    ada = _ada(c, w_ada[l], b_ada[l])
    shift1, scale1, gate1, shift2, scale2, gate2 = (
        ada[:, i * D:(i + 1) * D].reshape(B, 1, D) for i in range(6))
    proj = _proj(x2d, shift1, scale1, norm1_g[l].reshape(1, D), w_in[l].astype(BF16),
                 rc, rs1, rs2, S)
    lambda_init = 0.8 - 0.6 * float(np.exp(-0.3 * l))
    lam = (jnp.exp(jnp.sum(diff_lq1[l] * diff_lk1[l])) - jnp.exp(jnp.sum(diff_lq2[l] * diff_lk2[l]))
           + lambda_init).reshape(1).astype(F32)
    o_a = _attn(proj, lam, diff_subln_g[l].reshape(1, DA_V_DIM), B, S, lambda_init)
    o_b = _hgrn(proj, hgrn_lb_fwd, hgrn_lb_bwd, hgrn_norm_g[l].reshape(1, HG_DIM), B, S, l)
    x1, h2, qp = _mix(o_a, o_b, proj, x2d, gate1, shift2, scale2, norm2_g[l].reshape(1, D),
                      w_branch_attn[l].astype(BF16), w_branch_hgrn[l].astype(BF16),
                      w_out[l].astype(BF16), peer_wq[l].astype(BF16), S)
    e_t, g_t = _topk(qp, peer_keys1[l], peer_keys2[l])
    out = _peer(e_t.T, h2, g_t, x1, gate2, final_norm_g.reshape(1, D),
                _pack_expert_tables(peer_u[l], peer_v[l]), S)
    return out.reshape(B, S, D)
```

```python
import functools

import numpy as np
import jax
import jax.numpy as jnp
from jax import lax
from jax.experimental import pallas as pl
from jax.experimental.pallas import tpu as pltpu

F32 = jnp.float32
BF16 = jnp.bfloat16

EPS = 1e-6
DA_HEADS = 8
DA_QK_DIM = 64
DA_V_DIM = 128
ROPE_DIM = DA_QK_DIM // 4
ROPE_HALF = ROPE_DIM // 2
ROPE_THETA = 500000.0
HG_HEADS = 8
HG_DIM = 128
HG_CHUNK = 64
PEER_HEADS = 8
PEER_N_KEYS = 128
PEER_HALF = 128
PEER_TOPK = 16
PEER_KK = PEER_HEADS * PEER_TOPK

LANES = 128
VMEM_LIMIT = 56 * 1024 * 1024

COL_DA_Q, COL_DA_K, COL_DA_V = 0, 8, 16
COL_HG_Q, COL_HG_FF, COL_HG_FB, COL_HG_I, COL_HG_G = 24, 32, 40, 48, 56

_NT = (((1,), (1,)), ((), ()))
_TN = (((0,), (0,)), ((), ()))


def _params(*sem):
    return pltpu.CompilerParams(dimension_semantics=sem, vmem_limit_bytes=VMEM_LIMIT)


def _silu(x):
    return x * jax.nn.sigmoid(x)


def _ada_kernel(c_ref, w_ref, b_ref, o_ref):
    s = _silu(c_ref[...])
    s_hi = s.astype(BF16)
    s_lo = (s - s_hi.astype(F32)).astype(BF16)
    w = w_ref[...]
    w_hi = w.astype(BF16)
    w_lo = (w - w_hi.astype(F32)).astype(BF16)
    acc = jnp.dot(s_hi, w_hi, preferred_element_type=F32)
    acc += jnp.dot(s_lo, w_hi, preferred_element_type=F32)
    acc += jnp.dot(s_hi, w_lo, preferred_element_type=F32)
    o_ref[...] = acc + b_ref[...]


def _ada(c, w, b):
    B, D = c.shape
    N = w.shape[1]
    tn = 1024
    return pl.pallas_call(
        _ada_kernel,
        out_shape=jax.ShapeDtypeStruct((B, N), F32),
        grid=(N // tn,),
        in_specs=[pl.BlockSpec((B, D), lambda j: (0, 0)),
                  pl.BlockSpec((D, tn), lambda j: (0, j)),
                  pl.BlockSpec((1, tn), lambda j: (0, j))],
        out_specs=pl.BlockSpec((B, tn), lambda j: (0, j)),
        compiler_params=_params("arbitrary"),
        name="ada",
    )(c, w, b.reshape(1, N))


def _modulated_norm(xf, g, shift, scale):
    y = xf * lax.rsqrt(jnp.mean(xf * xf, axis=-1, keepdims=True) + EPS) * g
    return y * (1.0 + scale) + shift


def _proj_kernel(x_ref, shift_ref, scale_ref, g_ref, w_ref, rc_ref, rs1_ref, rs2_ref, o_ref, h_ref):
    j = pl.program_id(1)

    @pl.when(j == 0)
    def _():
        h_ref[...] = _modulated_norm(x_ref[...], g_ref[...], shift_ref[0], scale_ref[0]).astype(BF16)

    acc = jnp.dot(h_ref[...], w_ref[...], preferred_element_type=F32)
    tn = acc.shape[1]

    def store_rotated(mult):
        rc, rs1, rs2 = rc_ref[...], rs1_ref[...], rs2_ref[...]
        for cb in range(tn // LANES):
            t = acc[:, cb * LANES:(cb + 1) * LANES]
            r = (t * rc + pltpu.roll(t, ROPE_HALF, axis=1) * rs1
                 + pltpu.roll(t, LANES - ROPE_HALF, axis=1) * rs2)
            o_ref[:, cb * LANES:(cb + 1) * LANES] = (r * mult).astype(o_ref.dtype)

    @pl.when(j == 0)
    def _():
        store_rotated(DA_QK_DIM ** -0.5)

    @pl.when(j == 1)
    def _():
        store_rotated(1.0)

    @pl.when(j >= 2)
    def _():
        o_ref[...] = acc.astype(o_ref.dtype)


def _proj(x2d, shift, scale, g, w_bf16, rc, rs1, rs2, seq):
    T, D = x2d.shape
    N = w_bf16.shape[1]
    tm, tn = min(1024, seq), 1024
    assert seq % tm == 0 and N % tn == 0 and tn == DA_HEADS * 2 * DA_QK_DIM
    bidx = lambda i, j: ((i * tm) // seq, 0, 0)
    return pl.pallas_call(
        _proj_kernel,
        out_shape=jax.ShapeDtypeStruct((T, N), BF16),
        grid=(T // tm, N // tn),
        in_specs=[pl.BlockSpec((tm, D), lambda i, j: (i, 0)),
                  pl.BlockSpec((1, 1, D), bidx),
                  pl.BlockSpec((1, 1, D), bidx),
                  pl.BlockSpec((1, D), lambda i, j: (0, 0)),
                  pl.BlockSpec((D, tn), lambda i, j: (0, j)),
                  pl.BlockSpec((tm, LANES), lambda i, j: (i, 0)),
                  pl.BlockSpec((tm, LANES), lambda i, j: (i, 0)),
                  pl.BlockSpec((tm, LANES), lambda i, j: (i, 0))],
        out_specs=pl.BlockSpec((tm, tn), lambda i, j: (i, j)),
        scratch_shapes=[pltpu.VMEM((tm, D), BF16)],
        compiler_params=_params("arbitrary", "arbitrary"),
        name="proj",
    )(x2d, shift, scale, g, w_bf16, rc, rs1, rs2)


def _rope_lane_tables(positions):
    B, S = positions.shape
    T = B * S
    inv = ROPE_THETA ** (-jnp.arange(0, ROPE_DIM, 2, dtype=F32) / ROPE_DIM)
    ang = positions.astype(F32).reshape(T, 1) * inv
    cos, sin = jnp.cos(ang), jnp.sin(ang)
    one = jnp.ones((T, DA_QK_DIM - ROPE_DIM), F32)
    zero8 = jnp.zeros((T, ROPE_HALF), F32)
    zero48 = jnp.zeros((T, DA_QK_DIM - ROPE_DIM), F32)
    rc = jnp.concatenate([cos, cos, one], axis=1)
    rs1 = jnp.concatenate([zero8, sin, zero48], axis=1)
    rs2 = jnp.concatenate([-sin, zero8, zero48], axis=1)
    rep = LANES // DA_QK_DIM
    return jnp.tile(rc, (1, rep)), jnp.tile(rs1, (1, rep)), jnp.tile(rs2, (1, rep))


def _attn_kernel(lam_ref, q_ref, k_ref, v_ref, g_ref, o_ref, *, out_scale):
    q = q_ref[...]
    k = k_ref[...]
    v = v_ref[...]
    lane = lax.broadcasted_iota(jnp.int32, q.shape, 1)
    zero = jnp.zeros_like(q)

    def one_map(qm):
        s = lax.dot_general(qm, k, _NT, preferred_element_type=F32)
        e = jnp.exp(s - jnp.max(s, axis=-1, keepdims=True))
        l = jnp.sum(e, axis=-1, keepdims=True)
        return jnp.dot(e.astype(v.dtype), v, preferred_element_type=F32) / l

    o1 = one_map(jnp.where(lane < DA_QK_DIM, q, zero))
    o2 = one_map(jnp.where(lane >= DA_QK_DIM, q, zero))
    o = o1 - lam_ref[0] * o2
    o = o * lax.rsqrt(jnp.mean(o * o, axis=-1, keepdims=True) + EPS) * g_ref[...]
    o_ref[...] = (o * out_scale).astype(o_ref.dtype)


def _attn(proj, lam, subln_g, batch, seq, lambda_init):
    T = proj.shape[0]
    tq = 256
    nq = seq // tq
    return pl.pallas_call(
        functools.partial(_attn_kernel, out_scale=1.0 - lambda_init),
        out_shape=jax.ShapeDtypeStruct((T, DA_HEADS * DA_V_DIM), BF16),
        grid=(batch, DA_HEADS, nq),
        in_specs=[pl.BlockSpec(memory_space=pltpu.SMEM),
                  pl.BlockSpec((tq, LANES), lambda b, h, i: (b * nq + i, COL_DA_Q + h)),
                  pl.BlockSpec((seq, LANES), lambda b, h, i: (b, COL_DA_K + h)),
                  pl.BlockSpec((seq, LANES), lambda b, h, i: (b, COL_DA_V + h)),
                  pl.BlockSpec((1, LANES), lambda b, h, i: (0, 0))],
        out_specs=pl.BlockSpec((tq, LANES), lambda b, h, i: (b * nq + i, h)),
        compiler_params=_params("arbitrary", "arbitrary", "arbitrary"),
        name="attn",
    )(lam, proj, proj, proj, subln_g)


def _lower_bound(tbl, layer):
    e = jnp.exp(tbl - jnp.max(tbl, axis=0, keepdims=True))
    sm = e / jnp.sum(e, axis=0, keepdims=True)
    return jnp.sum(sm[1:layer + 2], axis=0, keepdims=True)


HG_SCAN_ROWS = 256


def _hgrn_kernel(q_ref, ff_ref, fb_ref, i_ref, g_ref, tf_ref, tb_ref, ng_ref, o_ref,
                 ut_ref, st_ref, dec_ref, *, layer):
    L = HG_CHUNK
    S = q_ref.shape[0]
    C = S // L
    R = min(HG_SCAN_ROWS, S)
    row = lax.broadcasted_iota(jnp.int32, (R, R), 0)
    col = lax.broadcasted_iota(jnp.int32, (R, R), 1)
    same_chunk = (row // L) == (col // L)
    row_l = lax.broadcasted_iota(jnp.int32, (L, L), 0)
    col_l = lax.broadcasted_iota(jnp.int32, (L, L), 1)

    q3 = _silu(q_ref[...].astype(F32)).reshape(C, L, HG_DIM)
    v3 = i_ref[...].reshape(C, L, HG_DIM)
    v3t = jnp.swapaxes(v3.astype(F32), 1, 2).astype(BF16)

    def direction(raw_ref, tbl_ref, forward):
        lb = _lower_bound(tbl_ref[...], layer)
        f = lb + (1.0 - lb) * jax.nn.sigmoid(raw_ref[...].astype(F32))
        kk3 = (1.0 - f).reshape(C, L, HG_DIM)
        lf = jnp.log(f)
        hi = lf.astype(BF16)
        lo = (lf - hi.astype(F32)).astype(BF16)
        hl = jnp.concatenate([hi, lo], axis=1)
        tri = (same_chunk & ((row >= col) if forward else (row <= col))).astype(BF16)
        parts = []
        for j in range(S // R):
            bs = jnp.dot(tri, hl[j * R:(j + 1) * R], preferred_element_type=F32)
            parts.append(bs[:, :HG_DIM] + bs[:, HG_DIM:])
        b3 = jnp.concatenate(parts, axis=0).reshape(C, L, HG_DIM)
        ref_row, last_row = (L // 2 - 1, L - 1) if forward else (L // 2, 0)
        r = b3[:, ref_row:ref_row + 1, :]
        bl = b3[:, last_row:last_row + 1, :]
        qi = (q3 * jnp.exp(b3 - r)).astype(BF16)
        ki = (kk3 * jnp.exp(r - b3)).astype(BF16)
        a = jnp.einsum('ctd,csd->cts', qi, ki, preferred_element_type=F32)
        mask = (row_l >= col_l) if forward else (row_l <= col_l)
        a = jnp.where(mask[None], a, 0.0)
        o = jnp.einsum('cts,csv->ctv', a.astype(BF16), v3, preferred_element_type=F32)
        ku = (kk3 * jnp.exp(bl - b3)).astype(BF16)
        ut_ref[...] = jnp.einsum('cvs,csd->cvd', v3t, ku, preferred_element_type=F32)
        dec_ref[...] = jnp.exp(bl)

        def scan(i, state):
            c = i if forward else C - 1 - i
            st_ref[c] = state.astype(BF16)
            return state * dec_ref[c] + ut_ref[c]

        lax.fori_loop(0, C, scan, jnp.zeros((HG_DIM, HG_DIM), F32))
        qb = (q3 * jnp.exp(b3)).astype(BF16)
        return o + jnp.einsum('ctd,cvd->ctv', qb, st_ref[...], preferred_element_type=F32)

    o = direction(ff_ref, tf_ref, True) + direction(fb_ref, tb_ref, False)
    o = o.reshape(S, HG_DIM)
    o = o * lax.rsqrt(jnp.mean(o * o, axis=-1, keepdims=True) + EPS) * ng_ref[...]
    o_ref[...] = (o * _silu(g_ref[...].astype(F32))).astype(o_ref.dtype)


def _hgrn(proj, tbl_f, tbl_b, norm_g, batch, seq, layer):
    T = proj.shape[0]
    nrow = tbl_f.shape[0]
    C = seq // HG_CHUNK
    col = lambda off: pl.BlockSpec((seq, LANES), lambda b, h: (b, off + h))
    tbl = pl.BlockSpec((nrow, LANES), lambda b, h: (0, h))
    return pl.pallas_call(
        functools.partial(_hgrn_kernel, layer=layer),
        out_shape=jax.ShapeDtypeStruct((T, HG_HEADS * HG_DIM), BF16),
        grid=(batch, HG_HEADS),
        in_specs=[col(COL_HG_Q), col(COL_HG_FF), col(COL_HG_FB), col(COL_HG_I), col(COL_HG_G),
                  tbl, tbl, pl.BlockSpec((1, LANES), lambda b, h: (0, 0))],
        out_specs=pl.BlockSpec((seq, LANES), lambda b, h: (b, h)),
        scratch_shapes=[pltpu.VMEM((C, HG_DIM, HG_DIM), F32), pltpu.VMEM((C, HG_DIM, HG_DIM), BF16),
                        pltpu.VMEM((C, 1, HG_DIM), F32)],
        compiler_params=_params("arbitrary", "arbitrary"),
        name="hgrn",
    )(proj, proj, proj, proj, proj, tbl_f, tbl_b, norm_g)


def _mix_kernel(oa_ref, ob_ref, ga_ref, gb_ref, x_ref, gate_ref, shift_ref, scale_ref, g2_ref,
                wa_ref, wb_ref, wo_ref, wq_ref, x1_ref, h2_ref, qp_ref):
    ya = jnp.dot(oa_ref[...], wa_ref[...], preferred_element_type=F32)
    yb = jnp.dot(ob_ref[...], wb_ref[...], preferred_element_type=F32)
    y = (jax.nn.sigmoid(ga_ref[...].astype(F32)) * ya
         + jax.nn.sigmoid(gb_ref[...].astype(F32)) * yb)
    z = jnp.dot(y.astype(BF16), wo_ref[...], preferred_element_type=F32)
    x1 = x_ref[...] + gate_ref[0] * z
    x1_ref[...] = x1
    h2 = _modulated_norm(x1, g2_ref[...], shift_ref[0], scale_ref[0])
    h2_ref[...] = h2
    qp_ref[...] = jnp.dot(h2.astype(BF16), wq_ref[...], preferred_element_type=F32)


def _mix(o_a, o_b, proj, x2d, gate1, shift2, scale2, g2, wa, wb, wo, wq, seq):
    T, D = x2d.shape
    tm = 256
    gcol = proj.shape[1] // D - 2
    bidx = lambda i: ((i * tm) // seq, 0, 0)
    const = lambda shape: pl.BlockSpec(shape, lambda i: (0, 0), pipeline_mode=pl.Buffered(1))
    row = lambda w: pl.BlockSpec((tm, w), lambda i: (i, 0))
    return pl.pallas_call(
        _mix_kernel,
        out_shape=(jax.ShapeDtypeStruct((T, D), F32), jax.ShapeDtypeStruct((T, D), F32),
                   jax.ShapeDtypeStruct((T, wq.shape[1]), F32)),
        grid=(T // tm,),
        in_specs=[row(o_a.shape[1]), row(o_b.shape[1]),
                  pl.BlockSpec((tm, D), lambda i: (i, gcol)),
                  pl.BlockSpec((tm, D), lambda i: (i, gcol + 1)),
                  row(D),
                  pl.BlockSpec((1, 1, D), bidx), pl.BlockSpec((1, 1, D), bidx),
                  pl.BlockSpec((1, 1, D), bidx),
                  pl.BlockSpec((1, D), lambda i: (0, 0)),
                  const(wa.shape), const(wb.shape), const(wo.shape), const(wq.shape)],
        out_specs=(row(D), row(D), row(wq.shape[1])),
        compiler_params=_params("arbitrary"),
        name="mix",
    )(o_a, o_b, proj, proj, x2d, gate1, shift2, scale2, g2, wa, wb, wo, wq)


_BIG = 1e9


def _take_top(s, order, payload, n):
    vals, picked = [], []
    for _ in range(n):
        m = jnp.max(s, axis=0, keepdims=True)
        first = jnp.min(jnp.where(s == m, order, _BIG), axis=0, keepdims=True)
        hit = order == first
        vals.append(m)
        picked.append(first if payload is None else
                      jnp.sum(jnp.where(hit, payload, 0.0), axis=0, keepdims=True))
        s = jnp.where(hit, -jnp.inf, s)
    return vals, picked


def _staircase(v1, i1, v2, i2):
    k = PEER_TOPK
    W = v1[0].shape[1]
    row = lax.broadcasted_iota(jnp.int32, (8, W), 0)
    rowf = row.astype(F32)
    v1c, i1c = jnp.concatenate(v1, axis=0), jnp.concatenate(i1, axis=0)
    v2c, i2c = jnp.concatenate(v2, axis=0), jnp.concatenate(i2, axis=0)
    v2lo, i2lo = v2c[0:8], i2c[0:8]

    def sel3(x5, x6, x7):
        return jnp.where(row < 2, x5, jnp.where(row < 4, x6, x7))

    def shifted(x, n):
        return pltpu.roll(x, n, axis=0)

    pieces = [
        (v1[0], i1[0], v2lo, i2lo, rowf, None),
        (v1[0], i1[0], v2c[8:16], i2c[8:16], 8.0 + rowf, None),
        (v1[1], i1[1], v2lo, i2lo, k + rowf, None),
        (v1[2], i1[2], v2lo, i2lo, 2 * k + rowf, row < 5),
        (jnp.where(row < 4, v1[3], v1[4]), jnp.where(row < 4, i1[3], i1[4]),
         jnp.where(row < 4, v2lo, shifted(v2lo, 4)), jnp.where(row < 4, i2lo, shifted(i2lo, 4)),
         jnp.where(row < 4, 3 * k + rowf, 4 * k - 4 + rowf), row < 7),
        (sel3(v1[5], v1[6], v1[7]), sel3(i1[5], i1[6], i1[7]),
         sel3(v2lo, shifted(v2lo, 2), shifted(v2lo, 4)), sel3(i2lo, shifted(i2lo, 2), shifted(i2lo, 4)),
         sel3(5 * k + rowf, 6 * k - 2 + rowf, 7 * k - 4 + rowf), row < 6),
        (v1c[8:16], i1c[8:16], v2[0], i2[0], (8.0 + rowf) * k, None),
    ]
    cand, order, ids = [], [], []
    for va, ia, vb, ib, flat, valid in pieces:
        c = va + vb
        if valid is not None:
            c = jnp.where(valid, c, -jnp.inf)
            flat = jnp.where(valid, flat, _BIG)
        cand.append(c)
        order.append(flat)
        ids.append(ia * PEER_N_KEYS + ib)
    return (jnp.concatenate(cand, axis=0), jnp.concatenate(order, axis=0),
            jnp.concatenate(ids, axis=0))


def _topk_kernel(qp_ref, k1_ref, k2_ref, e_ref, g_ref):
    qp = qp_ref[...]
    s1 = lax.dot_general(k1_ref[0], qp[:, :PEER_HALF], _NT, preferred_element_type=F32)
    s2 = lax.dot_general(k2_ref[0], qp[:, PEER_HALF:], _NT, preferred_element_type=F32)
    key_rank = lax.broadcasted_iota(jnp.int32, s1.shape, 0).astype(F32)
    v1, i1 = _take_top(s1, key_rank, None, PEER_TOPK)
    v2, i2 = _take_top(s2, key_rank, None, PEER_TOPK)
    cand, order, ids = _staircase(v1, i1, v2, i2)
    sc, e = _take_top(cand, order, ids, PEER_TOPK)
    sc = jnp.concatenate(sc, axis=0)
    ex = jnp.exp(sc - sc[0:1, :])
    e_ref[...] = jnp.concatenate(e, axis=0).astype(jnp.int32)
    g = ex / jnp.sum(ex, axis=0, keepdims=True)
    for j in range(g_ref.shape[0]):
        g_ref[j] = g[:, j * LANES:(j + 1) * LANES]


def _topk(qp, keys1, keys2):
    T = qp.shape[0]
    tt = 256
    return pl.pallas_call(
        _topk_kernel,
        out_shape=(jax.ShapeDtypeStruct((PEER_KK, T), jnp.int32),
                   jax.ShapeDtypeStruct((T // LANES, PEER_KK, LANES), F32)),
        grid=(T // tt, PEER_HEADS),
        in_specs=[pl.BlockSpec((tt, 2 * PEER_HALF), lambda i, h: (i, h)),
                  pl.BlockSpec((1, PEER_N_KEYS, PEER_HALF), lambda i, h: (h, 0, 0)),
                  pl.BlockSpec((1, PEER_N_KEYS, PEER_HALF), lambda i, h: (h, 0, 0))],
        out_specs=(pl.BlockSpec((PEER_TOPK, tt), lambda i, h: (h, i)),
                   pl.BlockSpec((tt // LANES, PEER_TOPK, LANES), lambda i, h: (i, h, 0))),
        compiler_params=_params("arbitrary", "arbitrary"),
        name="topk",
    )(qp, keys1, keys2)


PEER_TB = 256
PEER_SLOTS = 16
PEER_BF16_TERMS = 4


def _bf16_bits(x):
    b = lax.bitcast_convert_type(x, jnp.uint32)
    return (b + jnp.uint32(0x7FFF) + ((b >> 16) & jnp.uint32(1))) >> 16


def _pack_words(x):
    half = x.shape[1] // 2
    return (_bf16_bits(x[:, half:]) << 16) | _bf16_bits(x[:, :half])


def _pack_kernel(u_ref, v_ref, o_ref):
    half = u_ref.shape[1] // 2
    o_ref[:, 0, :half] = _pack_words(u_ref[...])
    o_ref[:, 0, half:] = _pack_words(v_ref[...])


def _pack_expert_tables(u, v):
    N, D = u.shape
    tr = 256
    return pl.pallas_call(
        _pack_kernel,
        out_shape=jax.ShapeDtypeStruct((N, 1, D), jnp.uint32),
        grid=(N // tr,),
        in_specs=[pl.BlockSpec((tr, D), lambda i: (i, 0)), pl.BlockSpec((tr, D), lambda i: (i, 0))],
        out_specs=pl.BlockSpec((tr, 1, D), lambda i: (i, 0, 0)),
        compiler_params=_params("arbitrary"),
        name="pack",
    )(u, v)


def _peer_kernel(e_ref, h2_ref, gt_ref, x1_ref, gate_ref, fg_ref, w_hbm, o_ref,
                 wbuf, sem, acc_ref, xw_ref):
    D = h2_ref.shape[1]
    half = D // 2
    half_blocks = [slice(cb * LANES, (cb + 1) * LANES) for cb in range(half // LANES)]
    high_bits = jnp.uint32(0xFFFF0000)

    def unpack(words):
        return (lax.bitcast_convert_type(words << 16, F32),
                lax.bitcast_convert_type(words & high_bits, F32))

    def issue(t, slot):
        for k in range(PEER_KK):
            pltpu.make_async_copy(w_hbm.at[e_ref[t, k]], wbuf.at[slot, pl.ds(k, 1)],
                                  sem.at[slot]).start(priority=k % 2)

    def wait(slot):
        pltpu.make_async_copy(w_hbm.at[pl.ds(0, PEER_KK), 0], wbuf.at[slot], sem.at[slot]).wait()

    lane = lax.broadcasted_iota(jnp.int32, (PEER_KK, LANES), 1)

    def expert_mlp(t, slot):
        xw = xw_ref[pl.ds(t, 1), :]
        part = jnp.zeros((PEER_KK, LANES), F32)
        for g0 in range(0, len(half_blocks), PEER_BF16_TERMS):
            prods = []
            for cols in half_blocks[g0:g0 + PEER_BF16_TERMS]:
                xb = pltpu.bitcast(jnp.broadcast_to(xw[:, cols], (PEER_KK, LANES)), BF16)
                prods.append(pltpu.bitcast(wbuf[slot, :, cols], BF16) * xb)
            while len(prods) > 1:
                prods = [a + b for a, b in zip(prods[::2], prods[1::2])]
            s_lo, s_hi = unpack(pltpu.bitcast(prods[0], jnp.uint32))
            part += s_lo + s_hi
        dots = jnp.sum(part, axis=1, keepdims=True)
        gcol = jnp.sum(jnp.where(lane == t % LANES, gt_ref[t // LANES], 0.0), axis=1, keepdims=True)
        a = 0.5 * dots * (1.0 + lax.erf(dots * (2.0 ** -0.5))) * gcol
        a_b = jnp.broadcast_to(a, (PEER_KK, LANES))
        lo_pieces, hi_pieces = [], []
        for cols in half_blocks:
            v_lo, v_hi = unpack(wbuf[slot, :, half + cols.start:half + cols.stop])
            lo_pieces.append(jnp.sum(a_b * v_lo, axis=0, keepdims=True))
            hi_pieces.append(jnp.sum(a_b * v_hi, axis=0, keepdims=True))
        return jnp.concatenate(lo_pieces + hi_pieces, axis=1)

    for slot in range(PEER_SLOTS):
        issue(slot, slot)

    xw_ref[...] = _pack_words(h2_ref[...])

    def steady(g, carry):
        for slot in range(PEER_SLOTS):
            t = g * PEER_SLOTS + slot
            wait(slot)
            row = expert_mlp(t, slot)
            issue(t + PEER_SLOTS, slot)
            acc_ref[pl.ds(t, 1), :] = row
        return carry

    n_groups = PEER_TB // PEER_SLOTS
    lax.fori_loop(0, n_groups - 1, steady, 0)
    for slot in range(PEER_SLOTS):
        t = (n_groups - 1) * PEER_SLOTS + slot
        wait(slot)
        acc_ref[pl.ds(t, 1), :] = expert_mlp(t, slot)
    x2 = x1_ref[...] + gate_ref[0] * acc_ref[...]
    o_ref[...] = x2 * lax.rsqrt(jnp.mean(x2 * x2, axis=-1, keepdims=True) + EPS) * fg_ref[...]


def _peer(e_tok, h2, g_t, x1, gate2, final_g, w_packed, seq):
    T, D = x1.shape
    tb = PEER_TB
    bidx = lambda i: ((i * tb) // seq, 0, 0)
    row = pl.BlockSpec((tb, D), lambda i: (i, 0))
    return pl.pallas_call(
        _peer_kernel,
        out_shape=jax.ShapeDtypeStruct((T, D), F32),
        grid=(T // tb,),
        in_specs=[pl.BlockSpec((tb, PEER_KK), lambda i: (i, 0), memory_space=pltpu.SMEM),
                  row,
                  pl.BlockSpec((tb // LANES, PEER_KK, LANES), lambda i: (i, 0, 0)),
                  row,
                  pl.BlockSpec((1, 1, D), bidx),
                  pl.BlockSpec((1, D), lambda i: (0, 0)),
                  pl.BlockSpec(memory_space=pl.ANY)],
        out_specs=row,
        scratch_shapes=[pltpu.VMEM((PEER_SLOTS, PEER_KK, D), jnp.uint32),
                        pltpu.SemaphoreType.DMA((PEER_SLOTS,)),
                        pltpu.VMEM((tb, D), F32),
                        pltpu.VMEM((tb, D // 2), jnp.uint32)],
        compiler_params=_params("arbitrary"),
        name="peer",
    )(e_tok, h2, g_t, x1, gate2, final_g, w_packed)


def kernel(x, c, positions, w_ada, b_ada, norm1_g, w_in, diff_lq1, diff_lk1, diff_lq2, diff_lk2,
           diff_subln_g, hgrn_lb_fwd, hgrn_lb_bwd, hgrn_norm_g, w_branch_attn, w_branch_hgrn,
           w_out, norm2_g, peer_wq, peer_keys1, peer_keys2, peer_u, peer_v, final_norm_g):
    B, S, D = x.shape
    T = B * S
    depth = w_ada.shape[0]
    assert depth == 1, "the fused final-norm epilogue assumes a single layer"
    rc, rs1, rs2 = _rope_lane_tables(positions)
    x2d = x.reshape(T, D)
    l = 0
    ada = _ada(c, w_ada[l], b_ada[l])
    shift1, scale1, gate1, shift2, scale2, gate2 = (
        ada[:, i * D:(i + 1) * D].reshape(B, 1, D) for i in range(6))
    proj = _proj(x2d, shift1, scale1, norm1_g[l].reshape(1, D), w_in[l].astype(BF16),
                 rc, rs1, rs2, S)
    lambda_init = 0.8 - 0.6 * float(np.exp(-0.3 * l))
    lam = (jnp.exp(jnp.sum(diff_lq1[l] * diff_lk1[l])) - jnp.exp(jnp.sum(diff_lq2[l] * diff_lk2[l]))
           + lambda_init).reshape(1).astype(F32)
    o_a = _attn(proj, lam, diff_subln_g[l].reshape(1, DA_V_DIM), B, S, lambda_init)
    o_b = _hgrn(proj, hgrn_lb_fwd, hgrn_lb_bwd, hgrn_norm_g[l].reshape(1, HG_DIM), B, S, l)
    x1, h2, qp = _mix(o_a, o_b, proj, x2d, gate1, shift2, scale2, norm2_g[l].reshape(1, D),
                      w_branch_attn[l].astype(BF16), w_branch_hgrn[l].astype(BF16),
                      w_out[l].astype(BF16), peer_wq[l].astype(BF16), S)
    e_t, g_t = _topk(qp, peer_keys1[l], peer_keys2[l])
    out = _peer(e_t.T, h2, g_t, x1, gate2, final_norm_g.reshape(1, D),
                _pack_expert_tables(peer_u[l], peer_v[l]), S)
    return out.reshape(B, S, D)
```

```python
import functools

import numpy as np
import jax
import jax.numpy as jnp
from jax import lax
from jax.experimental import pallas as pl
from jax.experimental.pallas import tpu as pltpu

F32 = jnp.float32
BF16 = jnp.bfloat16

EPS = 1e-6
DA_HEADS = 8
DA_QK_DIM = 64
DA_V_DIM = 128
ROPE_DIM = DA_QK_DIM // 4
ROPE_HALF = ROPE_DIM // 2
ROPE_THETA = 500000.0
HG_HEADS = 8
HG_DIM = 128
HG_CHUNK = 64
PEER_HEADS = 8
PEER_N_KEYS = 128
PEER_HALF = 128
PEER_TOPK = 16
PEER_KK = PEER_HEADS * PEER_TOPK

LANES = 128
VMEM_LIMIT = 56 * 1024 * 1024

COL_DA_Q, COL_DA_K, COL_DA_V = 0, 8, 16
COL_HG_Q, COL_HG_FF, COL_HG_FB, COL_HG_I, COL_HG_G = 24, 32, 40, 48, 56

_NT = (((1,), (1,)), ((), ()))
_TN = (((0,), (0,)), ((), ()))


def _params(*sem):
    return pltpu.CompilerParams(dimension_semantics=sem, vmem_limit_bytes=VMEM_LIMIT)


def _silu(x):
    return x * jax.nn.sigmoid(x)


def _ada_kernel(c_ref, w_ref, b_ref, o_ref):
    s = _silu(c_ref[...])
    s_hi = s.astype(BF16)
    s_lo = (s - s_hi.astype(F32)).astype(BF16)
    w = w_ref[...]
    w_hi = w.astype(BF16)
    w_lo = (w - w_hi.astype(F32)).astype(BF16)
    acc = jnp.dot(s_hi, w_hi, preferred_element_type=F32)
    acc += jnp.dot(s_lo, w_hi, preferred_element_type=F32)
    acc += jnp.dot(s_hi, w_lo, preferred_element_type=F32)
    o_ref[...] = acc + b_ref[...]


def _ada(c, w, b):
    B, D = c.shape
    N = w.shape[1]
    tn = 1024
    return pl.pallas_call(
        _ada_kernel,
        out_shape=jax.ShapeDtypeStruct((B, N), F32),
        grid=(N // tn,),
        in_specs=[pl.BlockSpec((B, D), lambda j: (0, 0)),
                  pl.BlockSpec((D, tn), lambda j: (0, j)),
                  pl.BlockSpec((1, tn), lambda j: (0, j))],
        out_specs=pl.BlockSpec((B, tn), lambda j: (0, j)),
        compiler_params=_params("arbitrary"),
        name="ada",
    )(c, w, b.reshape(1, N))


def _modulated_norm(xf, g, shift, scale):
    y = xf * lax.rsqrt(jnp.mean(xf * xf, axis=-1, keepdims=True) + EPS) * g
    return y * (1.0 + scale) + shift


def _proj_kernel(x_ref, shift_ref, scale_ref, g_ref, w_ref, rc_ref, rs1_ref, rs2_ref, o_ref, h_ref):
    j = pl.program_id(1)

    @pl.when(j == 0)
    def _():
        h_ref[...] = _modulated_norm(x_ref[...], g_ref[...], shift_ref[0], scale_ref[0]).astype(BF16)

    acc = jnp.dot(h_ref[...], w_ref[...], preferred_element_type=F32)
    tn = acc.shape[1]

    def store_rotated(mult):
        rc, rs1, rs2 = rc_ref[...], rs1_ref[...], rs2_ref[...]
        for cb in range(tn // LANES):
            t = acc[:, cb * LANES:(cb + 1) * LANES]
            r = (t * rc + pltpu.roll(t, ROPE_HALF, axis=1) * rs1
                 + pltpu.roll(t, LANES - ROPE_HALF, axis=1) * rs2)
            o_ref[:, cb * LANES:(cb + 1) * LANES] = (r * mult).astype(o_ref.dtype)

    @pl.when(j == 0)
    def _():
        store_rotated(DA_QK_DIM ** -0.5)

    @pl.when(j == 1)
    def _():
        store_rotated(1.0)

    @pl.when(j >= 2)
    def _():
        o_ref[...] = acc.astype(o_ref.dtype)


def _proj(x2d, shift, scale, g, w_bf16, rc, rs1, rs2, seq):
    T, D = x2d.shape
    N = w_bf16.shape[1]
    tm, tn = min(1024, seq), 1024
    assert seq % tm == 0 and N % tn == 0 and tn == DA_HEADS * 2 * DA_QK_DIM
    bidx = lambda i, j: ((i * tm) // seq, 0, 0)
    return pl.pallas_call(
        _proj_kernel,
        out_shape=jax.ShapeDtypeStruct((T, N), BF16),
        grid=(T // tm, N // tn),
        in_specs=[pl.BlockSpec((tm, D), lambda i, j: (i, 0)),
                  pl.BlockSpec((1, 1, D), bidx),
                  pl.BlockSpec((1, 1, D), bidx),
                  pl.BlockSpec((1, D), lambda i, j: (0, 0)),
                  pl.BlockSpec((D, tn), lambda i, j: (0, j)),
                  pl.BlockSpec((tm, LANES), lambda i, j: (i, 0)),
                  pl.BlockSpec((tm, LANES), lambda i, j: (i, 0)),
                  pl.BlockSpec((tm, LANES), lambda i, j: (i, 0))],
        out_specs=pl.BlockSpec((tm, tn), lambda i, j: (i, j)),
        scratch_shapes=[pltpu.VMEM((tm, D), BF16)],
        compiler_params=_params("arbitrary", "arbitrary"),
        name="proj",
    )(x2d, shift, scale, g, w_bf16, rc, rs1, rs2)


def _rope_lane_tables(positions):
    B, S = positions.shape
    T = B * S
    inv = ROPE_THETA ** (-jnp.arange(0, ROPE_DIM, 2, dtype=F32) / ROPE_DIM)
    ang = positions.astype(F32).reshape(T, 1) * inv
    cos, sin = jnp.cos(ang), jnp.sin(ang)
    one = jnp.ones((T, DA_QK_DIM - ROPE_DIM), F32)
    zero8 = jnp.zeros((T, ROPE_HALF), F32)
    zero48 = jnp.zeros((T, DA_QK_DIM - ROPE_DIM), F32)
    rc = jnp.concatenate([cos, cos, one], axis=1)
    rs1 = jnp.concatenate([zero8, sin, zero48], axis=1)
    rs2 = jnp.concatenate([-sin, zero8, zero48], axis=1)
    rep = LANES // DA_QK_DIM
    return jnp.tile(rc, (1, rep)), jnp.tile(rs1, (1, rep)), jnp.tile(rs2, (1, rep))


def _attn_kernel(lam_ref, q_ref, k_ref, v_ref, g_ref, o_ref, *, out_scale):
    q = q_ref[...]
    k = k_ref[...]
    v = v_ref[...]
    lane = lax.broadcasted_iota(jnp.int32, q.shape, 1)
    zero = jnp.zeros_like(q)

    def one_map(qm):
        s = lax.dot_general(qm, k, _NT, preferred_element_type=F32)
        e = jnp.exp(s - jnp.max(s, axis=-1, keepdims=True))
        l = jnp.sum(e, axis=-1, keepdims=True)
        return jnp.dot(e.astype(v.dtype), v, preferred_element_type=F32) / l

    o1 = one_map(jnp.where(lane < DA_QK_DIM, q, zero))
    o2 = one_map(jnp.where(lane >= DA_QK_DIM, q, zero))
    o = o1 - lam_ref[0] * o2
    o = o * lax.rsqrt(jnp.mean(o * o, axis=-1, keepdims=True) + EPS) * g_ref[...]
    o_ref[...] = (o * out_scale).astype(o_ref.dtype)


def _attn(proj, lam, subln_g, batch, seq, lambda_init):
    T = proj.shape[0]
    tq = 256
    nq = seq // tq
    return pl.pallas_call(
        functools.partial(_attn_kernel, out_scale=1.0 - lambda_init),
        out_shape=jax.ShapeDtypeStruct((T, DA_HEADS * DA_V_DIM), BF16),
        grid=(batch, DA_HEADS, nq),
        in_specs=[pl.BlockSpec(memory_space=pltpu.SMEM),
                  pl.BlockSpec((tq, LANES), lambda b, h, i: (b * nq + i, COL_DA_Q + h)),
                  pl.BlockSpec((seq, LANES), lambda b, h, i: (b, COL_DA_K + h)),
                  pl.BlockSpec((seq, LANES), lambda b, h, i: (b, COL_DA_V + h)),
                  pl.BlockSpec((1, LANES), lambda b, h, i: (0, 0))],
        out_specs=pl.BlockSpec((tq, LANES), lambda b, h, i: (b * nq + i, h)),
        compiler_params=_params("arbitrary", "arbitrary", "arbitrary"),
        name="attn",
    )(lam, proj, proj, proj, subln_g)


def _lower_bound(tbl, layer):
    e = jnp.exp(tbl - jnp.max(tbl, axis=0, keepdims=True))
    sm = e / jnp.sum(e, axis=0, keepdims=True)
    return jnp.sum(sm[1:layer + 2], axis=0, keepdims=True)


HG_SCAN_ROWS = 256


def _hgrn_kernel(q_ref, ff_ref, fb_ref, i_ref, g_ref, tf_ref, tb_ref, ng_ref, o_ref,
                 ut_ref, st_ref, dec_ref, *, layer):
    L = HG_CHUNK
    S = q_ref.shape[0]
    C = S // L
    R = min(HG_SCAN_ROWS, S)
    row = lax.broadcasted_iota(jnp.int32, (R, R), 0)
    col = lax.broadcasted_iota(jnp.int32, (R, R), 1)
    same_chunk = (row // L) == (col // L)
    row_l = lax.broadcasted_iota(jnp.int32, (L, L), 0)
    col_l = lax.broadcasted_iota(jnp.int32, (L, L), 1)

    q3 = _silu(q_ref[...].astype(F32)).reshape(C, L, HG_DIM)
    v3 = i_ref[...].reshape(C, L, HG_DIM)
    v3t = jnp.swapaxes(v3.astype(F32), 1, 2).astype(BF16)

    def direction(raw_ref, tbl_ref, forward):
        lb = _lower_bound(tbl_ref[...], layer)
        f = lb + (1.0 - lb) * jax.nn.sigmoid(raw_ref[...].astype(F32))
        kk3 = (1.0 - f).reshape(C, L, HG_DIM)
        lf = jnp.log(f)
        hi = lf.astype(BF16)
        lo = (lf - hi.astype(F32)).astype(BF16)
        hl = jnp.concatenate([hi, lo], axis=1)
        tri = (same_chunk & ((row >= col) if forward else (row <= col))).astype(BF16)
        parts = []
        for j in range(S // R):
            bs = jnp.dot(tri, hl[j * R:(j + 1) * R], preferred_element_type=F32)
            parts.append(bs[:, :HG_DIM] + bs[:, HG_DIM:])
        b3 = jnp.concatenate(parts, axis=0).reshape(C, L, HG_DIM)
        ref_row, last_row = (L // 2 - 1, L - 1) if forward else (L // 2, 0)
        r = b3[:, ref_row:ref_row + 1, :]
        bl = b3[:, last_row:last_row + 1, :]
        qi = (q3 * jnp.exp(b3 - r)).astype(BF16)
        ki = (kk3 * jnp.exp(r - b3)).astype(BF16)
        a = jnp.einsum('ctd,csd->cts', qi, ki, preferred_element_type=F32)
        mask = (row_l >= col_l) if forward else (row_l <= col_l)
        a = jnp.where(mask[None], a, 0.0)
        o = jnp.einsum('cts,csv->ctv', a.astype(BF16), v3, preferred_element_type=F32)
        ku = (kk3 * jnp.exp(bl - b3)).astype(BF16)
        ut_ref[...] = jnp.einsum('cvs,csd->cvd', v3t, ku, preferred_element_type=F32)
        dec_ref[...] = jnp.exp(bl)

        def scan(i, state):
            c = i if forward else C - 1 - i
            st_ref[c] = state.astype(BF16)
            return state * dec_ref[c] + ut_ref[c]

        lax.fori_loop(0, C, scan, jnp.zeros((HG_DIM, HG_DIM), F32))
        qb = (q3 * jnp.exp(b3)).astype(BF16)
        return o + jnp.einsum('ctd,cvd->ctv', qb, st_ref[...], preferred_element_type=F32)

    o = direction(ff_ref, tf_ref, True) + direction(fb_ref, tb_ref, False)
    o = o.reshape(S, HG_DIM)
    o = o * lax.rsqrt(jnp.mean(o * o, axis=-1, keepdims=True) + EPS) * ng_ref[...]
    o_ref[...] = (o * _silu(g_ref[...].astype(F32))).astype(o_ref.dtype)


def _hgrn(proj, tbl_f, tbl_b, norm_g, batch, seq, layer):
    T = proj.shape[0]
    nrow = tbl_f.shape[0]
    C = seq // HG_CHUNK
    col = lambda off: pl.BlockSpec((seq, LANES), lambda b, h: (b, off + h))
    tbl = pl.BlockSpec((nrow, LANES), lambda b, h: (0, h))
    return pl.pallas_call(
        functools.partial(_hgrn_kernel, layer=layer),
        out_shape=jax.ShapeDtypeStruct((T, HG_HEADS * HG_DIM), BF16),
        grid=(batch, HG_HEADS),
        in_specs=[col(COL_HG_Q), col(COL_HG_FF), col(COL_HG_FB), col(COL_HG_I), col(COL_HG_G),
                  tbl, tbl, pl.BlockSpec((1, LANES), lambda b, h: (0, 0))],
        out_specs=pl.BlockSpec((seq, LANES), lambda b, h: (b, h)),
        scratch_shapes=[pltpu.VMEM((C, HG_DIM, HG_DIM), F32), pltpu.VMEM((C, HG_DIM, HG_DIM), BF16),
                        pltpu.VMEM((C, 1, HG_DIM), F32)],
        compiler_params=_params("arbitrary", "arbitrary"),
        name="hgrn",
    )(proj, proj, proj, proj, proj, tbl_f, tbl_b, norm_g)


def _mix_kernel(oa_ref, ob_ref, ga_ref, gb_ref, x_ref, gate_ref, shift_ref, scale_ref, g2_ref,
                wa_ref, wb_ref, wo_ref, wq_ref, x1_ref, xw_ref, qp_ref):
    ya = jnp.dot(oa_ref[...], wa_ref[...], preferred_element_type=F32)
    yb = jnp.dot(ob_ref[...], wb_ref[...], preferred_element_type=F32)
    y = (jax.nn.sigmoid(ga_ref[...].astype(F32)) * ya
         + jax.nn.sigmoid(gb_ref[...].astype(F32)) * yb)
    z = jnp.dot(y.astype(BF16), wo_ref[...], preferred_element_type=F32)
    x1 = x_ref[...] + gate_ref[0] * z
    x1_ref[...] = x1
    h2 = _modulated_norm(x1, g2_ref[...], shift_ref[0], scale_ref[0])
    xw_ref[...] = _pack_words(h2)
    qp_ref[...] = jnp.dot(h2.astype(BF16), wq_ref[...], preferred_element_type=F32)


def _mix(o_a, o_b, proj, x2d, gate1, shift2, scale2, g2, wa, wb, wo, wq, seq):
    T, D = x2d.shape
    tm = 256
    gcol = proj.shape[1] // D - 2
    bidx = lambda i: ((i * tm) // seq, 0, 0)
    const = lambda shape: pl.BlockSpec(shape, lambda i: (0, 0), pipeline_mode=pl.Buffered(1))
    row = lambda w: pl.BlockSpec((tm, w), lambda i: (i, 0))
    return pl.pallas_call(
        _mix_kernel,
        out_shape=(jax.ShapeDtypeStruct((T, D), F32), jax.ShapeDtypeStruct((T, D // 2), jnp.uint32),
                   jax.ShapeDtypeStruct((T, wq.shape[1]), F32)),
        grid=(T // tm,),
        in_specs=[row(o_a.shape[1]), row(o_b.shape[1]),
                  pl.BlockSpec((tm, D), lambda i: (i, gcol)),
                  pl.BlockSpec((tm, D), lambda i: (i, gcol + 1)),
                  row(D),
                  pl.BlockSpec((1, 1, D), bidx), pl.BlockSpec((1, 1, D), bidx),
                  pl.BlockSpec((1, 1, D), bidx),
                  pl.BlockSpec((1, D), lambda i: (0, 0)),
                  const(wa.shape), const(wb.shape), const(wo.shape), const(wq.shape)],
        out_specs=(row(D), row(D // 2), row(wq.shape[1])),
        compiler_params=_params("arbitrary"),
        name="mix",
    )(o_a, o_b, proj, proj, x2d, gate1, shift2, scale2, g2, wa, wb, wo, wq)


_BIG = 1e9


def _take_top(s, order, payload, n):
    vals, picked = [], []
    for _ in range(n):
        m = jnp.max(s, axis=0, keepdims=True)
        first = jnp.min(jnp.where(s == m, order, _BIG), axis=0, keepdims=True)
        hit = order == first
        vals.append(m)
        picked.append(first if payload is None else
                      jnp.sum(jnp.where(hit, payload, 0.0), axis=0, keepdims=True))
        s = jnp.where(hit, -jnp.inf, s)
    return vals, picked


def _staircase(v1, i1, v2, i2):
    k = PEER_TOPK
    W = v1[0].shape[1]
    row = lax.broadcasted_iota(jnp.int32, (8, W), 0)
    rowf = row.astype(F32)
    v1c, i1c = jnp.concatenate(v1, axis=0), jnp.concatenate(i1, axis=0)
    v2c, i2c = jnp.concatenate(v2, axis=0), jnp.concatenate(i2, axis=0)
    v2lo, i2lo = v2c[0:8], i2c[0:8]

    def sel3(x5, x6, x7):
        return jnp.where(row < 2, x5, jnp.where(row < 4, x6, x7))

    def shifted(x, n):
        return pltpu.roll(x, n, axis=0)

    pieces = [
        (v1[0], i1[0], v2lo, i2lo, rowf, None),
        (v1[0], i1[0], v2c[8:16], i2c[8:16], 8.0 + rowf, None),
        (v1[1], i1[1], v2lo, i2lo, k + rowf, None),
        (v1[2], i1[2], v2lo, i2lo, 2 * k + rowf, row < 5),
        (jnp.where(row < 4, v1[3], v1[4]), jnp.where(row < 4, i1[3], i1[4]),
         jnp.where(row < 4, v2lo, shifted(v2lo, 4)), jnp.where(row < 4, i2lo, shifted(i2lo, 4)),
         jnp.where(row < 4, 3 * k + rowf, 4 * k - 4 + rowf), row < 7),
        (sel3(v1[5], v1[6], v1[7]), sel3(i1[5], i1[6], i1[7]),
         sel3(v2lo, shifted(v2lo, 2), shifted(v2lo, 4)), sel3(i2lo, shifted(i2lo, 2), shifted(i2lo, 4)),
         sel3(5 * k + rowf, 6 * k - 2 + rowf, 7 * k - 4 + rowf), row < 6),
        (v1c[8:16], i1c[8:16], v2[0], i2[0], (8.0 + rowf) * k, None),
    ]
    cand, order, ids = [], [], []
    for va, ia, vb, ib, flat, valid in pieces:
        c = va + vb
        if valid is not None:
            c = jnp.where(valid, c, -jnp.inf)
            flat = jnp.where(valid, flat, _BIG)
        cand.append(c)
        order.append(flat)
        ids.append(ia * PEER_N_KEYS + ib)
    return (jnp.concatenate(cand, axis=0), jnp.concatenate(order, axis=0),
            jnp.concatenate(ids, axis=0))


def _topk_kernel(qp_ref, k1_ref, k2_ref, e_ref, g_ref):
    qp = qp_ref[...]
    s1 = lax.dot_general(k1_ref[0], qp[:, :PEER_HALF], _NT, preferred_element_type=F32)
    s2 = lax.dot_general(k2_ref[0], qp[:, PEER_HALF:], _NT, preferred_element_type=F32)
    key_rank = lax.broadcasted_iota(jnp.int32, s1.shape, 0).astype(F32)
    v1, i1 = _take_top(s1, key_rank, None, PEER_TOPK)
    v2, i2 = _take_top(s2, key_rank, None, PEER_TOPK)
    cand, order, ids = _staircase(v1, i1, v2, i2)
    sc, e = _take_top(cand, order, ids, PEER_TOPK)
    sc = jnp.concatenate(sc, axis=0)
    ex = jnp.exp(sc - sc[0:1, :])
    e_ref[...] = jnp.concatenate(e, axis=0).astype(jnp.int32)
    g = ex / jnp.sum(ex, axis=0, keepdims=True)
    for j in range(g_ref.shape[0]):
        g_ref[j] = g[:, j * LANES:(j + 1) * LANES]


def _topk(qp, keys1, keys2):
    T = qp.shape[0]
    tt = 256
    return pl.pallas_call(
        _topk_kernel,
        out_shape=(jax.ShapeDtypeStruct((PEER_KK, T), jnp.int32),
                   jax.ShapeDtypeStruct((T // LANES, PEER_KK, LANES), F32)),
        grid=(T // tt, PEER_HEADS),
        in_specs=[pl.BlockSpec((tt, 2 * PEER_HALF), lambda i, h: (i, h)),
                  pl.BlockSpec((1, PEER_N_KEYS, PEER_HALF), lambda i, h: (h, 0, 0)),
                  pl.BlockSpec((1, PEER_N_KEYS, PEER_HALF), lambda i, h: (h, 0, 0))],
        out_specs=(pl.BlockSpec((PEER_TOPK, tt), lambda i, h: (h, i)),
                   pl.BlockSpec((tt // LANES, PEER_TOPK, LANES), lambda i, h: (i, h, 0))),
        compiler_params=_params("arbitrary", "arbitrary"),
        name="topk",
    )(qp, keys1, keys2)


PEER_TB = 256
PEER_SLOTS = 8
PEER_BF16_TERMS = 4


def _bf16_bits(x):
    b = lax.bitcast_convert_type(x, jnp.uint32)
    return (b + jnp.uint32(0x7FFF) + ((b >> 16) & jnp.uint32(1))) >> 16


def _pack_words(x):
    half = x.shape[1] // 2
    return (_bf16_bits(x[:, half:]) << 16) | _bf16_bits(x[:, :half])


def _pack_kernel(u_ref, v_ref, o_ref):
    half = u_ref.shape[1] // 2
    o_ref[:, 0, :half] = _pack_words(u_ref[...])
    o_ref[:, 0, half:] = _pack_words(v_ref[...])


def _pack_expert_tables(u, v):
    N, D = u.shape
    tr = 256
    return pl.pallas_call(
        _pack_kernel,
        out_shape=jax.ShapeDtypeStruct((N, 1, D), jnp.uint32),
        grid=(N // tr,),
        in_specs=[pl.BlockSpec((tr, D), lambda i: (i, 0)), pl.BlockSpec((tr, D), lambda i: (i, 0))],
        out_specs=pl.BlockSpec((tr, 1, D), lambda i: (i, 0, 0)),
        compiler_params=_params("arbitrary"),
        name="pack",
    )(u, v)


def _peer_kernel(e_ref, xw_ref, gt_ref, x1_ref, gate_ref, fg_ref, w_hbm, o_ref,
                 wbuf, sem, acc_ref):
    D = x1_ref.shape[1]
    half = D // 2
    half_blocks = [slice(cb * LANES, (cb + 1) * LANES) for cb in range(half // LANES)]
    high_bits = jnp.uint32(0xFFFF0000)

    def unpack(words):
        return (lax.bitcast_convert_type(words << 16, F32),
                lax.bitcast_convert_type(words & high_bits, F32))

    def issue(t, slot):
        for k in range(PEER_KK):
            pltpu.make_async_copy(w_hbm.at[e_ref[t, k]], wbuf.at[slot, pl.ds(k, 1)],
                                  sem.at[slot]).start(priority=k % 2)

    def wait(slot):
        pltpu.make_async_copy(w_hbm.at[pl.ds(0, PEER_KK), 0], wbuf.at[slot], sem.at[slot]).wait()

    lane = lax.broadcasted_iota(jnp.int32, (PEER_KK, LANES), 1)

    def expert_mlp(t, slot):
        xw = xw_ref[pl.ds(t, 1), :]
        part = jnp.zeros((PEER_KK, LANES), F32)
        for g0 in range(0, len(half_blocks), PEER_BF16_TERMS):
            prods = []
            for cols in half_blocks[g0:g0 + PEER_BF16_TERMS]:
                xb = pltpu.bitcast(jnp.broadcast_to(xw[:, cols], (PEER_KK, LANES)), BF16)
                prods.append(pltpu.bitcast(wbuf[slot, :, cols], BF16) * xb)
            while len(prods) > 1:
                prods = [a + b for a, b in zip(prods[::2], prods[1::2])]
            s_lo, s_hi = unpack(pltpu.bitcast(prods[0], jnp.uint32))
            part += s_lo + s_hi
        dots = jnp.sum(part, axis=1, keepdims=True)
        gcol = jnp.sum(jnp.where(lane == t % LANES, gt_ref[t // LANES], 0.0), axis=1, keepdims=True)
        a = 0.5 * dots * (1.0 + lax.erf(dots * (2.0 ** -0.5))) * gcol
        a_b = jnp.broadcast_to(a, (PEER_KK, LANES))
        lo_pieces, hi_pieces = [], []
        for cols in half_blocks:
            v_lo, v_hi = unpack(wbuf[slot, :, half + cols.start:half + cols.stop])
            lo_pieces.append(jnp.sum(a_b * v_lo, axis=0, keepdims=True))
            hi_pieces.append(jnp.sum(a_b * v_hi, axis=0, keepdims=True))
        return jnp.concatenate(lo_pieces + hi_pieces, axis=1)

    for slot in range(PEER_SLOTS):
        issue(slot, slot)

    def steady(g, carry):
        for slot in range(PEER_SLOTS):
            t = g * PEER_SLOTS + slot
            wait(slot)
            row = expert_mlp(t, slot)
            issue(t + PEER_SLOTS, slot)
            acc_ref[pl.ds(t, 1), :] = row
        return carry

    n_groups = PEER_TB // PEER_SLOTS
    lax.fori_loop(0, n_groups - 1, steady, 0)
    for slot in range(PEER_SLOTS):
        t = (n_groups - 1) * PEER_SLOTS + slot
        wait(slot)
        acc_ref[pl.ds(t, 1), :] = expert_mlp(t, slot)
    x2 = x1_ref[...] + gate_ref[0] * acc_ref[...]
    o_ref[...] = x2 * lax.rsqrt(jnp.mean(x2 * x2, axis=-1, keepdims=True) + EPS) * fg_ref[...]


def _peer(e_tok, h2, g_t, x1, gate2, final_g, w_packed, seq):
    T, D = x1.shape
    tb = PEER_TB
    bidx = lambda i: ((i * tb) // seq, 0, 0)
    row = pl.BlockSpec((tb, D), lambda i: (i, 0))
    return pl.pallas_call(
        _peer_kernel,
        out_shape=jax.ShapeDtypeStruct((T, D), F32),
        grid=(T // tb,),
        in_specs=[pl.BlockSpec((tb, PEER_KK), lambda i: (i, 0), memory_space=pltpu.SMEM),
                  pl.BlockSpec((tb, D // 2), lambda i: (i, 0)),
                  pl.BlockSpec((tb // LANES, PEER_KK, LANES), lambda i: (i, 0, 0)),
                  row,
                  pl.BlockSpec((1, 1, D), bidx),
                  pl.BlockSpec((1, D), lambda i: (0, 0)),
                  pl.BlockSpec(memory_space=pl.ANY)],
        out_specs=row,
        scratch_shapes=[pltpu.VMEM((PEER_SLOTS, PEER_KK, D), jnp.uint32),
                        pltpu.SemaphoreType.DMA((PEER_SLOTS,)),
                        pltpu.VMEM((tb, D), F32)],
        compiler_params=_params("arbitrary"),
        name="peer",
    )(e_tok, h2, g_t, x1, gate2, final_g, w_packed)


def kernel(x, c, positions, w_ada, b_ada, norm1_g, w_in, diff_lq1, diff_lk1, diff_lq2, diff_lk2,
           diff_subln_g, hgrn_lb_fwd, hgrn_lb_bwd, hgrn_norm_g, w_branch_attn, w_branch_hgrn,
           w_out, norm2_g, peer_wq, peer_keys1, peer_keys2, peer_u, peer_v, final_norm_g):
    B, S, D = x.shape
    T = B * S
    depth = w_ada.shape[0]
    assert depth == 1, "the fused final-norm epilogue assumes a single layer"
    rc, rs1, rs2 = _rope_lane_tables(positions)
    x2d = x.reshape(T, D)
    l = 0
    ada = _ada(c, w_ada[l], b_ada[l])
    shift1, scale1, gate1, shift2, scale2, gate2 = (
        ada[:, i * D:(i + 1) * D].reshape(B, 1, D) for i in range(6))
    proj = _proj(x2d, shift1, scale1, norm1_g[l].reshape(1, D), w_in[l].astype(BF16),
                 rc, rs1, rs2, S)
    lambda_init = 0.8 - 0.6 * float(np.exp(-0.3 * l))
    lam = (jnp.exp(jnp.sum(diff_lq1[l] * diff_lk1[l])) - jnp.exp(jnp.sum(diff_lq2[l] * diff_lk2[l]))
           + lambda_init).reshape(1).astype(F32)
    o_a = _attn(proj, lam, diff_subln_g[l].reshape(1, DA_V_DIM), B, S, lambda_init)
    o_b = _hgrn(proj, hgrn_lb_fwd, hgrn_lb_bwd, hgrn_norm_g[l].reshape(1, HG_DIM), B, S, l)
    x1, h2, qp = _mix(o_a, o_b, proj, x2d, gate1, shift2, scale2, norm2_g[l].reshape(1, D),
                      w_branch_attn[l].astype(BF16), w_branch_hgrn[l].astype(BF16),
                      w_out[l].astype(BF16), peer_wq[l].astype(BF16), S)
    e_t, g_t = _topk(qp, peer_keys1[l], peer_keys2[l])
    out = _peer(e_t.T, h2, g_t, x1, gate2, final_norm_g.reshape(1, D),
                _pack_expert_tables(peer_u[l], peer_v[l]), S)
    return out.reshape(B, S, D)
```
